```python
import jax, jax.numpy as jnp
from jax import lax
import numpy as np

D_MODEL = 1024
BATCH = 8
SEQ = 4096
DEPTH = 1

CHUNK = 64
ATT_HEADS = 8
ATT_HEAD_DIM = 64
ATT_WIDTH = ATT_HEADS * ATT_HEAD_DIM
N_PREV_CHUNKS = 8
BAND_CHUNKS = N_PREV_CHUNKS + 1
REL_CLIP = 256
SSD_HEAD_DIM = 64
SSD_WIDTH = D_MODEL
SSD_HEADS = SSD_WIDTH // SSD_HEAD_DIM
SSD_GROUPS = 2
SSD_STATE = 128
SSD_CONV = 4
SSD_CHUNK = CHUNK
CONV_DIM = SSD_WIDTH + 2 * SSD_GROUPS * SSD_STATE
MIX_WIDTH = ATT_WIDTH + SSD_WIDTH
IN_PROJ = 3 * ATT_WIDTH + SSD_WIDTH + CONV_DIM + SSD_HEADS
SPLITS = [ATT_WIDTH, 2 * ATT_WIDTH, 3 * ATT_WIDTH,
          3 * ATT_WIDTH + SSD_WIDTH,
          3 * ATT_WIDTH + SSD_WIDTH + CONV_DIM]
D_FF = 4 * D_MODEL
EPS = 1e-5

kernel_name = "hymba_chunkattn_ssd_sqrelu_block"


def rmsnorm(x, w):
    xf = x.astype(jnp.float32)
    y = xf * lax.rsqrt(jnp.mean(xf * xf, axis=-1, keepdims=True) + EPS)
    return (y * w.astype(jnp.float32)).astype(x.dtype)


def chunk_band_attention(q, k, v, rel_bias):
    bsz, seq = q.shape[0], q.shape[1]
    nc = seq // CHUNK
    shp = (bsz, nc, CHUNK, ATT_HEADS, ATT_HEAD_DIM)
    q = q.reshape(shp)
    k = k.reshape(shp)
    v = v.reshape(shp)
    pad = ((0, 0), (N_PREV_CHUNKS, 0), (0, 0), (0, 0), (0, 0))
    kp = jnp.pad(k, pad)
    vp = jnp.pad(v, pad)
    kb = jnp.concatenate([kp[:, j:j + nc] for j in range(BAND_CHUNKS)], axis=2)
    vb = jnp.concatenate([vp[:, j:j + nc] for j in range(BAND_CHUNKS)], axis=2)
    scale = ATT_HEAD_DIM ** -0.5
    scores = jnp.einsum('bcqhd,bckhd->bhcqk', q, kb).astype(jnp.float32) * scale
    qi = jnp.arange(CHUNK)[:, None]
    kj = jnp.arange(BAND_CHUNKS * CHUNK)[None, :]
    rel = N_PREV_CHUNKS * CHUNK + qi - kj
    idx = jnp.clip(rel, -REL_CLIP, REL_CLIP) + REL_CLIP
    bias = jnp.transpose(rel_bias.astype(jnp.float32)[idx], (2, 0, 1))
    key_chunk = jnp.arange(nc)[:, None] - N_PREV_CHUNKS + jnp.arange(BAND_CHUNKS)[None, :]
    valid = jnp.repeat(key_chunk >= 0, CHUNK, axis=1)
    scores = scores + bias[None, :, None]
    scores = jnp.where(valid[None, None, :, None, :], scores, -1e30)
    probs = jax.nn.softmax(scores, axis=-1).astype(v.dtype)
    out = jnp.einsum('bhcqk,bckhd->bcqhd', probs, vb)
    return out.reshape(bsz, seq, ATT_WIDTH)


def causal_depthwise_conv(u, w, b):
    out = lax.conv_general_dilated(
        u, w[:, None, :].astype(u.dtype), window_strides=(1,),
        padding=[(SSD_CONV - 1, 0)], dimension_numbers=('NWC', 'WIO', 'NWC'),
        feature_group_count=u.shape[-1])
    return out + b.astype(u.dtype)


def ssd_scan(x, dt, A, Bm, Cm):
    bsz, seq = x.shape[0], x.shape[1]
    nc = seq // SSD_CHUNK
    L = SSD_CHUNK
    r = SSD_HEADS // SSD_GROUPS
    X = (x * dt[..., None]).reshape(bsz, nc, L, SSD_GROUPS, r, SSD_HEAD_DIM)
    a = jnp.moveaxis((dt * A).reshape(bsz, nc, L, SSD_GROUPS, r), 2, -1)
    a_cum = jnp.cumsum(a, axis=-1)
    Bc = Bm.reshape(bsz, nc, L, SSD_GROUPS, SSD_STATE)
    Cc = Cm.reshape(bsz, nc, L, SSD_GROUPS, SSD_STATE)
    tril = jnp.tril(jnp.ones((L, L), dtype=bool))
    seg = a_cum[..., :, None] - a_cum[..., None, :]
    decay_in = jnp.exp(jnp.where(tril, seg, -jnp.inf))
    cb = jnp.einsum('bclgn,bcsgn->bcgls', Cc, Bc)
    y_diag = jnp.einsum('bcgls,bcgrls,bcsgrp->bclgrp', cb, decay_in, X)
    decay_out = jnp.exp(a_cum[..., -1:] - a_cum)
    states = jnp.einsum('bcsgn,bcgrs,bcsgrp->bcgrpn', Bc, decay_out, X)
    chunk_decay = jnp.exp(a_cum[..., -1])

    def step(carry, inp):
        st, dec = inp
        return carry * dec[..., None, None] + st, carry

    init = jnp.zeros_like(states[:, 0])
    _, prev = lax.scan(step, init, (jnp.moveaxis(states, 1, 0), jnp.moveaxis(chunk_decay, 1, 0)))
    prev = jnp.moveaxis(prev, 0, 1)
    y_off = jnp.einsum('bclgn,bcgrpn,bcgrl->bclgrp', Cc, prev, jnp.exp(a_cum))
    return (y_diag + y_off).reshape(bsz, seq, SSD_HEADS, SSD_HEAD_DIM)


def ssd_mixer(z, xbc, dt_raw, conv_w, conv_b, dt_bias, a_log, d_skip, norm_w):
    bsz, seq = z.shape[0], z.shape[1]
    xbc = jax.nn.silu(causal_depthwise_conv(xbc, conv_w, conv_b)).astype(jnp.float32)
    xs = xbc[..., :SSD_WIDTH]
    Bm = xbc[..., SSD_WIDTH:SSD_WIDTH + SSD_GROUPS * SSD_STATE].reshape(bsz, seq, SSD_GROUPS, SSD_STATE)
    Cm = xbc[..., SSD_WIDTH + SSD_GROUPS * SSD_STATE:].reshape(bsz, seq, SSD_GROUPS, SSD_STATE)
    dt = jax.nn.softplus(dt_raw.astype(jnp.float32) + dt_bias.astype(jnp.float32))
    A = -jnp.exp(a_log.astype(jnp.float32))
    xh = xs.reshape(bsz, seq, SSD_HEADS, SSD_HEAD_DIM)
    y = ssd_scan(xh, dt, A, Bm, Cm) + d_skip.astype(jnp.float32)[:, None] * xh
    y = y.reshape(bsz, seq, SSD_WIDTH) * jax.nn.silu(z.astype(jnp.float32))
    yg = y.reshape(bsz, seq, SSD_GROUPS, SSD_WIDTH // SSD_GROUPS)
    yg = yg * lax.rsqrt(jnp.mean(yg * yg, axis=-1, keepdims=True) + EPS)
    y = yg.reshape(bsz, seq, SSD_WIDTH) * norm_w.astype(jnp.float32)
    return y.astype(z.dtype)


def setup_inputs(seed: int = 0) -> dict:
    key = jax.random.key(seed)
    ks = jax.random.split(key, 16)
    f32 = jnp.float32
    x = jax.random.normal(ks[0], (BATCH, SEQ, D_MODEL), f32)
    norm_mix_w = 1.0 + 0.05 * jax.random.normal(ks[1], (DEPTH, D_MODEL), f32)
    w_in = jax.random.normal(ks[2], (DEPTH, D_MODEL, IN_PROJ), f32) * D_MODEL ** -0.5
    rel_bias = 0.1 * jax.random.normal(ks[3], (DEPTH, 2 * REL_CLIP + 1, ATT_HEADS), f32)
    conv_w = jax.random.normal(ks[4], (DEPTH, SSD_CONV, CONV_DIM), f32) * SSD_CONV ** -0.5
    conv_b = 0.01 * jax.random.normal(ks[5], (DEPTH, CONV_DIM), f32)
    dt0 = jnp.exp(jax.random.uniform(ks[6], (DEPTH, SSD_HEADS), f32,
                                     minval=np.log(1e-3), maxval=np.log(1e-1)))
    dt_bias = dt0 + jnp.log(-jnp.expm1(-dt0))
    a_log = jnp.log(jax.random.uniform(ks[7], (DEPTH, SSD_HEADS), f32, minval=1.0, maxval=16.0))
    d_skip = 1.0 + 0.1 * jax.random.normal(ks[8], (DEPTH, SSD_HEADS), f32)
    ssd_norm_w = 1.0 + 0.05 * jax.random.normal(ks[9], (DEPTH, SSD_WIDTH), f32)
    w_out = jax.random.normal(ks[10], (DEPTH, MIX_WIDTH, D_MODEL), f32) * MIX_WIDTH ** -0.5
    norm_mlp_w = 1.0 + 0.05 * jax.random.normal(ks[11], (DEPTH, D_MODEL), f32)
    w_ff1 = jax.random.normal(ks[12], (DEPTH, D_MODEL, D_FF), f32) * D_MODEL ** -0.5
    w_ff2 = jax.random.normal(ks[13], (DEPTH, D_FF, D_MODEL), f32) * D_FF ** -0.5
    norm_final_w = 1.0 + 0.05 * jax.random.normal(ks[14], (D_MODEL,), f32)
    return {"x": x, "norm_mix_w": norm_mix_w, "w_in": w_in, "rel_bias": rel_bias,
            "conv_w": conv_w, "conv_b": conv_b, "dt_bias": dt_bias, "a_log": a_log,
            "d_skip": d_skip, "ssd_norm_w": ssd_norm_w, "w_out": w_out,
            "norm_mlp_w": norm_mlp_w, "w_ff1": w_ff1, "w_ff2": w_ff2,
            "norm_final_w": norm_final_w}


def reference(x, norm_mix_w, w_in, rel_bias, conv_w, conv_b, dt_bias, a_log, d_skip,
              ssd_norm_w, w_out, norm_mlp_w, w_ff1, w_ff2, norm_final_w):
    h = x
    for i in range(DEPTH):
        hn = rmsnorm(h, norm_mix_w[i])
        proj = hn @ w_in[i]
        q, k, v, z, xbc, dt_raw = jnp.split(proj, SPLITS, axis=-1)
        att = chunk_band_attention(q, k, v, rel_bias[i])
        ssd = ssd_mixer(z, xbc, dt_raw, conv_w[i], conv_b[i], dt_bias[i],
                        a_log[i], d_skip[i], ssd_norm_w[i])
        h = h + jnp.concatenate([att, ssd], axis=-1) @ w_out[i]
        hn = rmsnorm(h, norm_mlp_w[i])
        h = h + jnp.square(jax.nn.relu(hn @ w_ff1[i])) @ w_ff2[i]
    return rmsnorm(h, norm_final_w)
```

```python
import functools

import jax
import jax.numpy as jnp
from jax import lax
from jax.experimental import pallas as pl
from jax.experimental.pallas import tpu as pltpu

F32 = jnp.float32
BF16 = jnp.bfloat16

D_MODEL = 1024
CHUNK = 64
ATT_HEADS = 8
ATT_HEAD_DIM = 64
ATT_WIDTH = ATT_HEADS * ATT_HEAD_DIM
N_PREV_CHUNKS = 8
REL_CLIP = 256
SSD_HEAD_DIM = 64
SSD_WIDTH = D_MODEL
SSD_HEADS = SSD_WIDTH // SSD_HEAD_DIM
SSD_GROUPS = 2
SSD_STATE = 128
SSD_CONV = 4
CONV_DIM = SSD_WIDTH + 2 * SSD_GROUPS * SSD_STATE
MAIN_PROJ = 3 * ATT_WIDTH + SSD_WIDTH + CONV_DIM
D_FF = 4 * D_MODEL
EPS = 1e-5
NEG = -1e30

LANES = 128
SUBLANES = 8

PROJ_TM = 512
ATT_TQ = 256
ATT_BAND = ATT_TQ + N_PREV_CHUNKS * CHUNK
SSD_L = 128
FFN_TM = 512
VMEM_LIMIT = 56 * 1024 * 1024


def _rms(x, w):
    ms = jnp.mean(x * x, axis=-1, keepdims=True)
    return x * lax.rsqrt(ms + EPS) * w


def _in_proj_kernel(x_ref, nw_ref, wm_ref, wdt_ref,
                    q_ref, k_ref, v_ref, z_ref, xbc_ref, dt_ref):
    hn = _rms(x_ref[...], nw_ref[...]).astype(BF16)

    def seg(lo, hi):
        return jnp.dot(hn, wm_ref[:, lo:hi], preferred_element_type=F32).astype(BF16)

    a = ATT_WIDTH
    q_ref[...] = seg(0, a)
    k_ref[...] = seg(a, 2 * a)
    v_ref[...] = seg(2 * a, 3 * a)
    z_ref[...] = seg(3 * a, 3 * a + SSD_WIDTH)
    xbc_ref[...] = seg(3 * a + SSD_WIDTH, MAIN_PROJ)
    dt_ref[...] = jnp.dot(hn, wdt_ref[...], preferred_element_type=F32)


def _in_proj(x2, norm_w, w_main, w_dt):
    m = x2.shape[0]
    tm = PROJ_TM
    row = lambda i: (i, 0)
    const = lambda i: (0, 0)
    out_shape = (
        jax.ShapeDtypeStruct((m, ATT_WIDTH), BF16),
        jax.ShapeDtypeStruct((m, ATT_WIDTH), BF16),
        jax.ShapeDtypeStruct((m, ATT_WIDTH), BF16),
        jax.ShapeDtypeStruct((m, SSD_WIDTH), BF16),
        jax.ShapeDtypeStruct((m, CONV_DIM), BF16),
        jax.ShapeDtypeStruct((m, LANES), F32),
    )
    return pl.pallas_call(
        _in_proj_kernel,
        out_shape=out_shape,
        grid=(m // tm,),
        in_specs=[
            pl.BlockSpec((tm, D_MODEL), row),
            pl.BlockSpec((1, D_MODEL), const),
            pl.BlockSpec((D_MODEL, MAIN_PROJ), const),
            pl.BlockSpec((D_MODEL, LANES), const),
        ],
        out_specs=(
            pl.BlockSpec((tm, ATT_WIDTH), row),
            pl.BlockSpec((tm, ATT_WIDTH), row),
            pl.BlockSpec((tm, ATT_WIDTH), row),
            pl.BlockSpec((tm, SSD_WIDTH), row),
            pl.BlockSpec((tm, CONV_DIM), row),
            pl.BlockSpec((tm, LANES), row),
        ),
        compiler_params=pltpu.CompilerParams(
            dimension_semantics=("parallel",), vmem_limit_bytes=VMEM_LIMIT),
        name="in_proj",
    )(x2, norm_w, w_main, w_dt)


def _attn_kernel(q_ref, k_ref, v_ref, tab_ref, o_ref, kband, vband):
    i = pl.program_id(1)
    tq = ATT_TQ
    prev = ATT_BAND - tq

    @pl.when(i == 0)
    def _():
        kband[0:prev, :] = jnp.zeros((prev, ATT_WIDTH), BF16)
        vband[0:prev, :] = jnp.zeros((prev, ATT_WIDTH), BF16)

    @pl.when(i > 0)
    def _():
        for r in range(prev // tq):
            kband[r * tq:(r + 1) * tq, :] = kband[(r + 1) * tq:(r + 2) * tq, :]
            vband[r * tq:(r + 1) * tq, :] = vband[(r + 1) * tq:(r + 2) * tq, :]

    kband[prev:ATT_BAND, :] = k_ref[0]
    vband[prev:ATT_BAND, :] = v_ref[0]

    kj = lax.broadcasted_iota(jnp.int32, (1, ATT_BAND), 1)
    colmask = jnp.where(kj < prev - tq * i, NEG, 0.0).astype(F32)

    for h in range(ATT_HEADS):
        sl = slice(h * ATT_HEAD_DIM, (h + 1) * ATT_HEAD_DIM)
        qh = q_ref[0, :, sl]
        kh = kband[:, sl]
        s = lax.dot_general(qh, kh, (((1,), (1,)), ((), ())),
                            preferred_element_type=F32)
        s = s + tab_ref[h] + colmask
        m = jnp.max(s, axis=-1, keepdims=True)
        e = jnp.exp(s - m)
        l = jnp.sum(e, axis=-1, keepdims=True)
        o = jnp.dot(e.astype(BF16), vband[:, sl], preferred_element_type=F32)
        o_ref[0, :, sl] = (o / l).astype(BF16)


def _attention(q, k, v, table):
    b, s, _ = q.shape
    tq = ATT_TQ
    blk = lambda bi, i: (bi, i, 0)
    return pl.pallas_call(
        _attn_kernel,
        out_shape=jax.ShapeDtypeStruct((b, s, ATT_WIDTH), BF16),
        grid=(b, s // tq),
        in_specs=[
            pl.BlockSpec((1, tq, ATT_WIDTH), blk),
            pl.BlockSpec((1, tq, ATT_WIDTH), blk),
            pl.BlockSpec((1, tq, ATT_WIDTH), blk),
            pl.BlockSpec((ATT_HEADS, tq, ATT_BAND), lambda bi, i: (0, 0, 0)),
        ],
        out_specs=pl.BlockSpec((1, tq, ATT_WIDTH), blk),
        scratch_shapes=[
            pltpu.VMEM((ATT_BAND, ATT_WIDTH), BF16),
            pltpu.VMEM((ATT_BAND, ATT_WIDTH), BF16),
        ],
        compiler_params=pltpu.CompilerParams(
            dimension_semantics=("parallel", "arbitrary"),
            vmem_limit_bytes=VMEM_LIMIT),
        name="band_attn",
    )(q, k, v, table)


def _bias_table(rel_bias):
    qi = jnp.arange(ATT_TQ)[:, None]
    kj = jnp.arange(ATT_BAND)[None, :]
    rel = qi + N_PREV_CHUNKS * CHUNK - kj
    idx = jnp.clip(rel, -REL_CLIP, REL_CLIP) + REL_CLIP
    qc = qi // CHUNK
    kc = kj // CHUNK
    valid = (kc >= qc) & (kc <= qc + N_PREV_CHUNKS)
    bias = jnp.transpose(rel_bias.astype(F32)[idx], (2, 0, 1))
    return jnp.where(valid[None], bias, NEG)


def _split3(v):
    v1 = v.astype(BF16).astype(F32)
    r1 = v - v1
    v2 = r1.astype(BF16).astype(F32)
    v3 = (r1 - v2).astype(BF16).astype(F32)
    return v1, v2, v3


def _expand_heads(v, e3_ref):
    lane = lax.broadcasted_iota(jnp.int32, v.shape, 1)
    v = jnp.where(lane < SSD_HEADS, v, 0.0)
    v1, v2, v3 = _split3(v)
    packed = v1 + pltpu.roll(v2, SSD_HEADS, 1) + pltpu.roll(v3, 2 * SSD_HEADS, 1)
    return jnp.dot(packed.astype(BF16), e3_ref[...], preferred_element_type=F32)


def _ssd_kernel(z_ref, xbc_ref, dt_ref, cw_ref, cb_ref, dtb_ref, alog_ref,
                dexp_ref, nw_ref, e3_ref, y_ref, ubuf, state, ybuf):
    j = pl.program_id(1)
    L = SSD_L
    hd = SSD_HEAD_DIM
    gw = SSD_WIDTH // SSD_GROUPS
    hpg = SSD_HEADS // SSD_GROUPS

    @pl.when(j == 0)
    def _():
        ubuf[0:SUBLANES, :] = jnp.zeros((SUBLANES, CONV_DIM), F32)
        state[...] = jnp.zeros(state.shape, F32)

    @pl.when(j > 0)
    def _():
        ubuf[0:SUBLANES, :] = ubuf[L:L + SUBLANES, :]

    ubuf[SUBLANES:L + SUBLANES, :] = xbc_ref[0].astype(F32)

    acc = jnp.broadcast_to(cb_ref[...], (L, CONV_DIM))
    for t in range(SSD_CONV):
        off = SUBLANES - (SSD_CONV - 1) + t
        acc = acc + cw_ref[t:t + 1, :] * ubuf[off:off + L, :]
    xc = acc * jax.nn.sigmoid(acc)
    xs = xc[:, :SSD_WIDTH]
    xs_bf = xs.astype(BF16)
    bm = xc[:, SSD_WIDTH:SSD_WIDTH + SSD_GROUPS * SSD_STATE]
    cm = xc[:, SSD_WIDTH + SSD_GROUPS * SSD_STATE:]

    u = dt_ref[0] + dtb_ref[...]
    dt = jnp.maximum(u, 0.0) + jnp.log1p(jnp.exp(-jnp.abs(u)))
    a = dt * (-jnp.exp(alog_ref[...]))
    row = lax.broadcasted_iota(jnp.int32, (L, L), 0)
    col = lax.broadcasted_iota(jnp.int32, (L, L), 1)
    causal = row >= col
    tril = causal.astype(F32)
    triu = (row <= col).astype(F32)
    hi = lax.Precision.HIGHEST
    a_cum = jnp.dot(tril, a, precision=hi, preferred_element_type=F32)
    a_cum_t = jnp.dot(a.T, triu, precision=hi, preferred_element_type=F32)
    dt_t = dt.T
    exp_acum = jnp.exp(a_cum)
    a_last = a_cum[L - 1:L, :]
    w = dt * jnp.exp(a_last - a_cum)
    xd = (xs * _expand_heads(w, e3_ref)).astype(BF16)
    cd = _expand_heads(jnp.broadcast_to(jnp.exp(a_last), (SUBLANES, LANES)), e3_ref)[0:1, :]

    for g in range(SSD_GROUPS):
        bg = bm[:, g * SSD_STATE:(g + 1) * SSD_STATE]
        cg = cm[:, g * SSD_STATE:(g + 1) * SSD_STATE]
        cb = lax.dot_general(cg.astype(BF16), bg.astype(BF16), (((1,), (1,)), ((), ())),
                             preferred_element_type=F32)
        sg = state[g]
        sg_bf = sg.astype(BF16)
        for r in range(hpg):
            h = g * hpg + r
            seg = a_cum[:, h:h + 1] - a_cum_t[h:h + 1, :]
            dec = jnp.where(causal, jnp.exp(seg), 0.0)
            mh = (cb * dec * dt_t[h:h + 1, :]).astype(BF16)
            ch = (cg * exp_acum[:, h:h + 1]).astype(BF16)
            lhs = jnp.concatenate([mh, ch], axis=1)
            rhs = jnp.concatenate([xs_bf[:, h * hd:(h + 1) * hd],
                                   sg_bf[:, r * hd:(r + 1) * hd]], axis=0)
            ybuf[:, h * hd:(h + 1) * hd] = jnp.dot(lhs, rhs, preferred_element_type=F32)
        s_new = jnp.dot(bg.T.astype(BF16), xd[:, g * gw:(g + 1) * gw],
                        preferred_element_type=F32)
        state[g] = sg * cd[:, g * gw:(g + 1) * gw] + s_new

    y = ybuf[...] + dexp_ref[...] * xs
    zf = z_ref[0].astype(F32)
    y = y * (zf * jax.nn.sigmoid(zf))
    for g in range(SSD_GROUPS):
        yg = y[:, g * gw:(g + 1) * gw]
        ms = jnp.mean(yg * yg, axis=-1, keepdims=True)
        y_ref[0, :, g * gw:(g + 1) * gw] = (
            yg * lax.rsqrt(ms + EPS) * nw_ref[:, g * gw:(g + 1) * gw]).astype(BF16)


def _ssd(z, xbc, dt_raw, conv_w, conv_b, dt_bias_p, a_log_p, d_exp, norm_w, e3):
    b, s, _ = z.shape
    L = SSD_L
    blk = lambda bi, j: (bi, j, 0)
    const = lambda bi, j: (0, 0)
    return pl.pallas_call(
        _ssd_kernel,
        out_shape=jax.ShapeDtypeStruct((b, s, SSD_WIDTH), BF16),
        grid=(b, s // L),
        in_specs=[
            pl.BlockSpec((1, L, SSD_WIDTH), blk),
            pl.BlockSpec((1, L, CONV_DIM), blk),
            pl.BlockSpec((1, L, LANES), blk),
            pl.BlockSpec((SSD_CONV, CONV_DIM), const),
            pl.BlockSpec((1, CONV_DIM), const),
            pl.BlockSpec((1, LANES), const),
            pl.BlockSpec((1, LANES), const),
            pl.BlockSpec((1, SSD_WIDTH), const),
            pl.BlockSpec((1, SSD_WIDTH), const),
            pl.BlockSpec((LANES, SSD_WIDTH), const),
        ],
        out_specs=pl.BlockSpec((1, L, SSD_WIDTH), blk),
        scratch_shapes=[
            pltpu.VMEM((L + SUBLANES, CONV_DIM), F32),
            pltpu.VMEM((SSD_GROUPS, SSD_STATE, SSD_WIDTH // SSD_GROUPS), F32),
            pltpu.VMEM((L, SSD_WIDTH), F32),
        ],
        compiler_params=pltpu.CompilerParams(
            dimension_semantics=("parallel", "arbitrary"),
            vmem_limit_bytes=VMEM_LIMIT),
        name="ssd_mixer",
    )(z, xbc, dt_raw, conv_w, conv_b, dt_bias_p, a_log_p, d_exp, norm_w, e3)


def _expand_matrix():
    r = jnp.arange(LANES)[:, None]
    c = jnp.arange(SSD_WIDTH)[None, :]
    hit = (r < 3 * SSD_HEADS) & ((r % SSD_HEADS) == c // SSD_HEAD_DIM)
    return hit.astype(BF16)


def _out_ffn_kernel(x_ref, att_ref, ssd_ref, woa_ref, wos_ref, nw_ref,
                    w1_ref, w2_ref, fw_ref, o_ref):
    h = (x_ref[...]
         + jnp.dot(att_ref[...], woa_ref[...], preferred_element_type=F32)
         + jnp.dot(ssd_ref[...], wos_ref[...], preferred_element_type=F32))
    hn = _rms(h, nw_ref[...]).astype(BF16)
    hid = jnp.dot(hn, w1_ref[...], preferred_element_type=F32)
    hid = jnp.square(jnp.maximum(hid, 0.0)).astype(BF16)
    h = h + jnp.dot(hid, w2_ref[...], preferred_element_type=F32)
    o_ref[...] = _rms(h, fw_ref[...])


def _out_ffn(x2, att2, ssd2, wo_a, wo_s, norm_w, w1, w2, final_w):
    m = x2.shape[0]
    tm = FFN_TM
    row = lambda i: (i, 0)
    const = lambda i: (0, 0)
    once = pl.Buffered(1)
    return pl.pallas_call(
        _out_ffn_kernel,
        out_shape=jax.ShapeDtypeStruct((m, D_MODEL), F32),
        grid=(m // tm,),
        in_specs=[
            pl.BlockSpec((tm, D_MODEL), row),
            pl.BlockSpec((tm, ATT_WIDTH), row),
            pl.BlockSpec((tm, SSD_WIDTH), row),
            pl.BlockSpec((ATT_WIDTH, D_MODEL), const, pipeline_mode=once),
            pl.BlockSpec((SSD_WIDTH, D_MODEL), const, pipeline_mode=once),
            pl.BlockSpec((1, D_MODEL), const),
            pl.BlockSpec((D_MODEL, D_FF), const, pipeline_mode=once),
            pl.BlockSpec((D_FF, D_MODEL), const, pipeline_mode=once),
            pl.BlockSpec((1, D_MODEL), const),
        ],
        out_specs=pl.BlockSpec((tm, D_MODEL), row),
        compiler_params=pltpu.CompilerParams(
            dimension_semantics=("parallel",), vmem_limit_bytes=VMEM_LIMIT),
        name="out_ffn",
    )(x2, att2, ssd2, wo_a, wo_s, norm_w, w1, w2, final_w)


def _layer(x, norm_mix_w, w_in, rel_bias, conv_w, conv_b, dt_bias, a_log, d_skip,
           ssd_norm_w, w_out, norm_mlp_w, w_ff1, w_ff2, out_norm_w):
    b, s, d = x.shape
    m = b * s
    x2 = x.reshape(m, d)

    scale = ATT_HEAD_DIM ** -0.5
    w_main = jnp.concatenate(
        [w_in[:, :ATT_WIDTH] * scale, w_in[:, ATT_WIDTH:MAIN_PROJ]], axis=1).astype(BF16)
    w_dt = jnp.pad(w_in[:, MAIN_PROJ:], ((0, 0), (0, LANES - SSD_HEADS))).astype(BF16)
    pad_h = lambda v: jnp.pad(v.astype(F32), (0, LANES - SSD_HEADS)).reshape(1, LANES)

    q, k, v, z, xbc, dt_raw = _in_proj(x2, norm_mix_w.reshape(1, d), w_main, w_dt)

    att = _attention(q.reshape(b, s, -1), k.reshape(b, s, -1), v.reshape(b, s, -1),
                     _bias_table(rel_bias))
    ssd = _ssd(z.reshape(b, s, -1), xbc.reshape(b, s, -1), dt_raw.reshape(b, s, -1),
               conv_w.astype(F32), conv_b.reshape(1, -1).astype(F32),
               pad_h(dt_bias), pad_h(a_log),
               jnp.repeat(d_skip.astype(F32), SSD_HEAD_DIM).reshape(1, -1),
               ssd_norm_w.reshape(1, -1).astype(F32), _expand_matrix())

    y = _out_ffn(x2, att.reshape(m, -1), ssd.reshape(m, -1),
                 w_out[:ATT_WIDTH].astype(BF16), w_out[ATT_WIDTH:].astype(BF16),
                 norm_mlp_w.reshape(1, d), w_ff1.astype(BF16), w_ff2.astype(BF16),
                 out_norm_w.reshape(1, d))
    return y.reshape(b, s, d)


def kernel(x, norm_mix_w, w_in, rel_bias, conv_w, conv_b, dt_bias, a_log, d_skip,
           ssd_norm_w, w_out, norm_mlp_w, w_ff1, w_ff2, norm_final_w):
    depth = w_in.shape[0]
    assert depth == 1, "final norm is fused into the single layer's last kernel"
    return _layer(x, norm_mix_w[0], w_in[0], rel_bias[0], conv_w[0], conv_b[0],
                  dt_bias[0], a_log[0], d_skip[0], ssd_norm_w[0], w_out[0],
                  norm_mlp_w[0], w_ff1[0], w_ff2[0], norm_final_w)
```

```python
import functools

import jax
import jax.numpy as jnp
from jax import lax
from jax.experimental import pallas as pl
from jax.experimental.pallas import tpu as pltpu

F32 = jnp.float32
BF16 = jnp.bfloat16

D_MODEL = 1024
CHUNK = 64
ATT_HEADS = 8
ATT_HEAD_DIM = 64
ATT_WIDTH = ATT_HEADS * ATT_HEAD_DIM
N_PREV_CHUNKS = 8
REL_CLIP = 256
SSD_HEAD_DIM = 64
SSD_WIDTH = D_MODEL
SSD_HEADS = SSD_WIDTH // SSD_HEAD_DIM
SSD_GROUPS = 2
SSD_STATE = 128
SSD_CONV = 4
CONV_DIM = SSD_WIDTH + 2 * SSD_GROUPS * SSD_STATE
MAIN_PROJ = 3 * ATT_WIDTH + SSD_WIDTH + CONV_DIM
D_FF = 4 * D_MODEL
EPS = 1e-5
NEG = -1e30

LANES = 128
SUBLANES = 8

PROJ_TM = 512
ATT_TQ = 256
ATT_BAND = ATT_TQ + N_PREV_CHUNKS * CHUNK
SSD_L = 128
FFN_TM = 512
VMEM_LIMIT = 56 * 1024 * 1024


def _rms(x, w):
    ms = jnp.mean(x * x, axis=-1, keepdims=True)
    return x * lax.rsqrt(ms + EPS) * w


def _in_proj_kernel(x_ref, nw_ref, wm_ref, wkt_ref, wdt_ref,
                    q_ref, kt_ref, v_ref, z_ref, xbc_ref, dt_ref):
    hn = _rms(x_ref[...], nw_ref[...]).astype(BF16)

    def seg(lo, hi):
        return jnp.dot(hn, wm_ref[:, lo:hi], preferred_element_type=F32).astype(BF16)

    a = ATT_WIDTH
    q_ref[...] = seg(0, a)
    kt_ref[...] = lax.dot_general(wkt_ref[...], hn, (((1,), (1,)), ((), ())),
                                  preferred_element_type=F32).astype(BF16)
    v_ref[...] = seg(a, 2 * a)
    z_ref[...] = seg(2 * a, 2 * a + SSD_WIDTH)
    xbc_ref[...] = seg(2 * a + SSD_WIDTH, MAIN_PROJ - a)
    dt_ref[...] = jnp.dot(hn, wdt_ref[...], preferred_element_type=F32)


def _in_proj(x2, norm_w, w_main, w_kt, w_dt):
    m = x2.shape[0]
    tm = PROJ_TM
    row = lambda i: (i, 0)
    const = lambda i: (0, 0)
    out_shape = (
        jax.ShapeDtypeStruct((m, ATT_WIDTH), BF16),
        jax.ShapeDtypeStruct((ATT_WIDTH, m), BF16),
        jax.ShapeDtypeStruct((m, ATT_WIDTH), BF16),
        jax.ShapeDtypeStruct((m, SSD_WIDTH), BF16),
        jax.ShapeDtypeStruct((m, CONV_DIM), BF16),
        jax.ShapeDtypeStruct((m, LANES), F32),
    )
    return pl.pallas_call(
        _in_proj_kernel,
        out_shape=out_shape,
        grid=(m // tm,),
        in_specs=[
            pl.BlockSpec((tm, D_MODEL), row),
            pl.BlockSpec((1, D_MODEL), const),
            pl.BlockSpec((D_MODEL, MAIN_PROJ - ATT_WIDTH), const),
            pl.BlockSpec((ATT_WIDTH, D_MODEL), const),
            pl.BlockSpec((D_MODEL, LANES), const),
        ],
        out_specs=(
            pl.BlockSpec((tm, ATT_WIDTH), row),
            pl.BlockSpec((ATT_WIDTH, tm), lambda i: (0, i)),
            pl.BlockSpec((tm, ATT_WIDTH), row),
            pl.BlockSpec((tm, SSD_WIDTH), row),
            pl.BlockSpec((tm, CONV_DIM), row),
            pl.BlockSpec((tm, LANES), row),
        ),
        compiler_params=pltpu.CompilerParams(
            dimension_semantics=("parallel",), vmem_limit_bytes=VMEM_LIMIT),
        name="in_proj",
    )(x2, norm_w, w_main, w_kt, w_dt)


ATT_SLOTS = ATT_BAND // ATT_TQ
PAIR = 2 * ATT_HEAD_DIM
LOG2E = 1.4426950408889634


def _bias_table_kernel(r_ref, o_ref):
    x = jnp.broadcast_to(r_ref[0], (ATT_TQ, ATT_BAND + ATT_TQ))
    rolled = pltpu.roll(x, 0, 1, stride=1, stride_axis=0)
    t = rolled[:, ATT_TQ:]
    qc = lax.broadcasted_iota(jnp.int32, (ATT_TQ, ATT_BAND), 0) // CHUNK
    kc = lax.broadcasted_iota(jnp.int32, (ATT_TQ, ATT_BAND), 1) // CHUNK
    valid = (kc >= qc) & (kc <= qc + N_PREV_CHUNKS)
    o_ref[0] = jnp.where(valid, t * LOG2E, NEG)


def _bias_table(rel_bias):
    rb = rel_bias.astype(F32).T
    n = ATT_BAND + ATT_TQ
    far = jnp.broadcast_to(rb[:, 2 * REL_CLIP:], (ATT_HEADS, n - 2 * REL_CLIP + 1))
    near = rb[:, 2 * REL_CLIP - 1:0:-1]
    r = jnp.concatenate([far, near], axis=1).reshape(ATT_HEADS, 1, n)
    return pl.pallas_call(
        _bias_table_kernel,
        out_shape=jax.ShapeDtypeStruct((ATT_HEADS, ATT_TQ, ATT_BAND), F32),
        grid=(ATT_HEADS,),
        in_specs=[pl.BlockSpec((1, 1, n), lambda h: (h, 0, 0))],
        out_specs=pl.BlockSpec((1, ATT_TQ, ATT_BAND), lambda h: (h, 0, 0)),
        compiler_params=pltpu.CompilerParams(dimension_semantics=("parallel",)),
        name="bias_table",
    )(r)


def _attn_kernel(q_ref, kt_ref, v_ref, tab_ref, o_ref, kring, vring, tabs):
    i = pl.program_id(1)
    tq = ATT_TQ
    lane = lax.broadcasted_iota(jnp.int32, (tq, PAIR), 1)
    low = lane < ATT_HEAD_DIM

    @pl.when(i == 0)
    def _():
        kring[...] = jnp.zeros(kring.shape, BF16)
        vring[...] = jnp.zeros(vring.shape, BF16)

    @pl.when(i < ATT_SLOTS)
    def _():
        for m in range(ATT_SLOTS):
            off = jnp.where(i - (ATT_SLOTS - 1) + m < 0, NEG, 0.0).astype(F32)
            for h in range(ATT_HEADS):
                tabs[h, :, m * tq:(m + 1) * tq] = tab_ref[h, :, m * tq:(m + 1) * tq] + off

    slot_new = lax.rem(i, ATT_SLOTS)
    hd = ATT_HEAD_DIM
    for h in range(ATT_HEADS):
        r0 = h * PAIR + (h % 2) * hd
        kring[slot_new, r0:r0 + hd, :] = kt_ref[h * hd:(h + 1) * hd, :]
    for p in range(ATT_HEADS // 2):
        vp = v_ref[0, :, p * PAIR:(p + 1) * PAIR]
        one = jnp.ones_like(vp)
        vring[slot_new, :, (2 * p) * PAIR:(2 * p + 1) * PAIR] = jnp.where(low, vp, one)
        vring[slot_new, :, (2 * p + 1) * PAIR:(2 * p + 2) * PAIR] = jnp.where(low, one, vp)

    slots = [lax.rem(i + 1 + m, ATT_SLOTS) for m in range(ATT_SLOTS)]

    def scores(h):
        p = h // 2
        qp = q_ref[0, :, p * PAIR:(p + 1) * PAIR]
        return [jnp.dot(qp, kring[slots[m], h * PAIR:(h + 1) * PAIR, :],
                        preferred_element_type=F32)
                + tabs[h, :, m * tq:(m + 1) * tq] for m in range(ATT_SLOTS)]

    s_next = scores(0)
    res = []
    for h in range(ATT_HEADS):
        s = s_next
        if h + 1 < ATT_HEADS:
            s_next = scores(h + 1)
        mx = functools.reduce(jnp.maximum, s)
        mx = jnp.max(mx, axis=-1, keepdims=True)
        acc = None
        for m in range(ATT_SLOTS):
            e = jnp.exp2(s[m] - mx).astype(BF16)
            pv = jnp.dot(e, vring[slots[m], :, h * PAIR:(h + 1) * PAIR],
                         preferred_element_type=F32)
            acc = pv if acc is None else acc + pv
        res.append(acc)
        if h % 2 == 1:
            p = h // 2
            num = jnp.where(low, res[0], res[1])
            den = jnp.where(low, pltpu.roll(res[0], ATT_HEAD_DIM, 1),
                            pltpu.roll(res[1], ATT_HEAD_DIM, 1))
            o_ref[0, :, p * PAIR:(p + 1) * PAIR] = (num / den).astype(BF16)
            res = []


def _attention(q, kt, v, table):
    b, s, _ = q.shape
    tq = ATT_TQ
    nblk = s // tq
    blk = lambda bi, i: (bi, i, 0)
    return pl.pallas_call(
        _attn_kernel,
        out_shape=jax.ShapeDtypeStruct((b, s, ATT_WIDTH), BF16),
        grid=(b, s // tq),
        in_specs=[
            pl.BlockSpec((1, tq, ATT_WIDTH), blk),
            pl.BlockSpec((ATT_WIDTH, tq), lambda bi, i: (0, bi * nblk + i)),
            pl.BlockSpec((1, tq, ATT_WIDTH), blk),
            pl.BlockSpec((ATT_HEADS, tq, ATT_BAND), lambda bi, i: (0, 0, 0),
                         pipeline_mode=pl.Buffered(1)),
        ],
        out_specs=pl.BlockSpec((1, tq, ATT_WIDTH), blk),
        scratch_shapes=[
            pltpu.VMEM((ATT_SLOTS, ATT_HEADS * PAIR, tq), BF16),
            pltpu.VMEM((ATT_SLOTS, tq, ATT_HEADS * PAIR), BF16),
            pltpu.VMEM((ATT_HEADS, tq, ATT_BAND), F32),
        ],
        compiler_params=pltpu.CompilerParams(
            dimension_semantics=("parallel", "arbitrary"),
            vmem_limit_bytes=VMEM_LIMIT),
        name="band_attn",
    )(q, kt, v, table)


def _split3(v):
    v1 = v.astype(BF16).astype(F32)
    r1 = v - v1
    v2 = r1.astype(BF16).astype(F32)
    v3 = (r1 - v2).astype(BF16).astype(F32)
    return v1, v2, v3


def _expand_heads(v, e3_ref):
    lane = lax.broadcasted_iota(jnp.int32, v.shape, 1)
    v = jnp.where(lane < SSD_HEADS, v, 0.0)
    v1, v2, v3 = _split3(v)
    packed = v1 + pltpu.roll(v2, SSD_HEADS, 1) + pltpu.roll(v3, 2 * SSD_HEADS, 1)
    return jnp.dot(packed.astype(BF16), e3_ref[...], preferred_element_type=F32)


def _ssd_kernel(z_ref, xbc_ref, dt_ref, cw_ref, cb_ref, dtb_ref, alog_ref,
                dexp_ref, nw_ref, e3_ref, y_ref, ubuf, state, ybuf):
    j = pl.program_id(1)
    L = SSD_L
    hd = SSD_HEAD_DIM
    gw = SSD_WIDTH // SSD_GROUPS
    hpg = SSD_HEADS // SSD_GROUPS

    @pl.when(j == 0)
    def _():
        ubuf[0:SUBLANES, :] = jnp.zeros((SUBLANES, CONV_DIM), F32)
        state[...] = jnp.zeros(state.shape, F32)

    @pl.when(j > 0)
    def _():
        ubuf[0:SUBLANES, :] = ubuf[L:L + SUBLANES, :]

    ubuf[SUBLANES:L + SUBLANES, :] = xbc_ref[0].astype(F32)

    acc = jnp.broadcast_to(cb_ref[...], (L, CONV_DIM))
    for t in range(SSD_CONV):
        off = SUBLANES - (SSD_CONV - 1) + t
        acc = acc + cw_ref[t:t + 1, :] * ubuf[off:off + L, :]
    xc = acc * jax.nn.sigmoid(acc)
    xs = xc[:, :SSD_WIDTH]
    xs_bf = xs.astype(BF16)
    bm = xc[:, SSD_WIDTH:SSD_WIDTH + SSD_GROUPS * SSD_STATE]
    cm = xc[:, SSD_WIDTH + SSD_GROUPS * SSD_STATE:]

    u = dt_ref[0] + dtb_ref[...]
    dt = jnp.maximum(u, 0.0) + jnp.log1p(jnp.exp(-jnp.abs(u)))
    a = dt * (-jnp.exp(alog_ref[...]))
    row = lax.broadcasted_iota(jnp.int32, (L, L), 0)
    col = lax.broadcasted_iota(jnp.int32, (L, L), 1)
    causal = row >= col
    tril = causal.astype(F32)
    triu = (row <= col).astype(F32)
    hi = lax.Precision.HIGHEST
    a_cum = jnp.dot(tril, a, precision=hi, preferred_element_type=F32)
    a_cum_t = jnp.dot(a.T, triu, precision=hi, preferred_element_type=F32)
    dt_t = dt.T
    exp_acum = jnp.exp(a_cum)
    a_last = a_cum[L - 1:L, :]
    w = dt * jnp.exp(a_last - a_cum)
    xd = (xs * _expand_heads(w, e3_ref)).astype(BF16)
    cd = _expand_heads(jnp.broadcast_to(jnp.exp(a_last), (SUBLANES, LANES)), e3_ref)[0:1, :]

    for g in range(SSD_GROUPS):
        bg = bm[:, g * SSD_STATE:(g + 1) * SSD_STATE]
        cg = cm[:, g * SSD_STATE:(g + 1) * SSD_STATE]
        cb = lax.dot_general(cg.astype(BF16), bg.astype(BF16), (((1,), (1,)), ((), ())),
                             preferred_element_type=F32)
        sg = state[g]
        sg_bf = sg.astype(BF16)
        for r in range(hpg):
            h = g * hpg + r
            seg = a_cum[:, h:h + 1] - a_cum_t[h:h + 1, :]
            dec = jnp.where(causal, jnp.exp(seg), 0.0)
            mh = (cb * dec * dt_t[h:h + 1, :]).astype(BF16)
            ch = (cg * exp_acum[:, h:h + 1]).astype(BF16)
            lhs = jnp.concatenate([mh, ch], axis=1)
            rhs = jnp.concatenate([xs_bf[:, h * hd:(h + 1) * hd],
                                   sg_bf[:, r * hd:(r + 1) * hd]], axis=0)
            ybuf[:, h * hd:(h + 1) * hd] = jnp.dot(lhs, rhs, preferred_element_type=F32)
        s_new = jnp.dot(bg.T.astype(BF16), xd[:, g * gw:(g + 1) * gw],
                        preferred_element_type=F32)
        state[g] = sg * cd[:, g * gw:(g + 1) * gw] + s_new

    y = ybuf[...] + dexp_ref[...] * xs
    zf = z_ref[0].astype(F32)
    y = y * (zf * jax.nn.sigmoid(zf))
    for g in range(SSD_GROUPS):
        yg = y[:, g * gw:(g + 1) * gw]
        ms = jnp.mean(yg * yg, axis=-1, keepdims=True)
        y_ref[0, :, g * gw:(g + 1) * gw] = (
            yg * lax.rsqrt(ms + EPS) * nw_ref[:, g * gw:(g + 1) * gw]).astype(BF16)


def _ssd(z, xbc, dt_raw, conv_w, conv_b, dt_bias_p, a_log_p, d_exp, norm_w, e3):
    b, s, _ = z.shape
    L = SSD_L
    blk = lambda bi, j: (bi, j, 0)
    const = lambda bi, j: (0, 0)
    return pl.pallas_call(
        _ssd_kernel,
        out_shape=jax.ShapeDtypeStruct((b, s, SSD_WIDTH), BF16),
        grid=(b, s // L),
        in_specs=[
            pl.BlockSpec((1, L, SSD_WIDTH), blk),
            pl.BlockSpec((1, L, CONV_DIM), blk),
            pl.BlockSpec((1, L, LANES), blk),
            pl.BlockSpec((SSD_CONV, CONV_DIM), const),
            pl.BlockSpec((1, CONV_DIM), const),
            pl.BlockSpec((1, LANES), const),
            pl.BlockSpec((1, LANES), const),
            pl.BlockSpec((1, SSD_WIDTH), const),
            pl.BlockSpec((1, SSD_WIDTH), const),
            pl.BlockSpec((LANES, SSD_WIDTH), const),
        ],
        out_specs=pl.BlockSpec((1, L, SSD_WIDTH), blk),
        scratch_shapes=[
            pltpu.VMEM((L + SUBLANES, CONV_DIM), F32),
            pltpu.VMEM((SSD_GROUPS, SSD_STATE, SSD_WIDTH // SSD_GROUPS), F32),
            pltpu.VMEM((L, SSD_WIDTH), F32),
        ],
        compiler_params=pltpu.CompilerParams(
            dimension_semantics=("parallel", "arbitrary"),
            vmem_limit_bytes=VMEM_LIMIT),
        name="ssd_mixer",
    )(z, xbc, dt_raw, conv_w, conv_b, dt_bias_p, a_log_p, d_exp, norm_w, e3)


def _expand_matrix():
    r = jnp.arange(LANES)[:, None]
    c = jnp.arange(SSD_WIDTH)[None, :]
    hit = (r < 3 * SSD_HEADS) & ((r % SSD_HEADS) == c // SSD_HEAD_DIM)
    return hit.astype(BF16)


def _out_ffn_kernel(x_ref, att_ref, ssd_ref, woa_ref, wos_ref, nw_ref,
                    w1_ref, w2_ref, fw_ref, o_ref):
    h = (x_ref[...]
         + jnp.dot(att_ref[...], woa_ref[...], preferred_element_type=F32)
         + jnp.dot(ssd_ref[...], wos_ref[...], preferred_element_type=F32))
    hn = _rms(h, nw_ref[...]).astype(BF16)
    hid = jnp.dot(hn, w1_ref[...], preferred_element_type=F32)
    hid = jnp.square(jnp.maximum(hid, 0.0)).astype(BF16)
    h = h + jnp.dot(hid, w2_ref[...], preferred_element_type=F32)
    o_ref[...] = _rms(h, fw_ref[...])


def _out_ffn(x2, att2, ssd2, wo_a, wo_s, norm_w, w1, w2, final_w):
    m = x2.shape[0]
    tm = FFN_TM
    row = lambda i: (i, 0)
    const = lambda i: (0, 0)
    once = pl.Buffered(1)
    return pl.pallas_call(
        _out_ffn_kernel,
        out_shape=jax.ShapeDtypeStruct((m, D_MODEL), F32),
        grid=(m // tm,),
        in_specs=[
            pl.BlockSpec((tm, D_MODEL), row),
            pl.BlockSpec((tm, ATT_WIDTH), row),
            pl.BlockSpec((tm, SSD_WIDTH), row),
            pl.BlockSpec((ATT_WIDTH, D_MODEL), const, pipeline_mode=once),
            pl.BlockSpec((SSD_WIDTH, D_MODEL), const, pipeline_mode=once),
            pl.BlockSpec((1, D_MODEL), const),
            pl.BlockSpec((D_MODEL, D_FF), const, pipeline_mode=once),
            pl.BlockSpec((D_FF, D_MODEL), const, pipeline_mode=once),
            pl.BlockSpec((1, D_MODEL), const),
        ],
        out_specs=pl.BlockSpec((tm, D_MODEL), row),
        compiler_params=pltpu.CompilerParams(
            dimension_semantics=("parallel",), vmem_limit_bytes=VMEM_LIMIT),
        name="out_ffn",
    )(x2, att2, ssd2, wo_a, wo_s, norm_w, w1, w2, final_w)


def _layer(x, norm_mix_w, w_in, rel_bias, conv_w, conv_b, dt_bias, a_log, d_skip,
           ssd_norm_w, w_out, norm_mlp_w, w_ff1, w_ff2, out_norm_w):
    b, s, d = x.shape
    m = b * s
    x2 = x.reshape(m, d)

    scale = ATT_HEAD_DIM ** -0.5 * LOG2E
    w_main = jnp.concatenate(
        [w_in[:, :ATT_WIDTH] * scale, w_in[:, 2 * ATT_WIDTH:MAIN_PROJ]], axis=1).astype(BF16)
    w_kt = w_in[:, ATT_WIDTH:2 * ATT_WIDTH].T.astype(BF16)
    w_dt = jnp.pad(w_in[:, MAIN_PROJ:], ((0, 0), (0, LANES - SSD_HEADS))).astype(BF16)
    pad_h = lambda v: jnp.pad(v.astype(F32), (0, LANES - SSD_HEADS)).reshape(1, LANES)

    q, kt, v, z, xbc, dt_raw = _in_proj(x2, norm_mix_w.reshape(1, d), w_main, w_kt, w_dt)

    att = _attention(q.reshape(b, s, -1), kt, v.reshape(b, s, -1), _bias_table(rel_bias))
    ssd = _ssd(z.reshape(b, s, -1), xbc.reshape(b, s, -1), dt_raw.reshape(b, s, -1),
               conv_w.astype(F32), conv_b.reshape(1, -1).astype(F32),
               pad_h(dt_bias), pad_h(a_log),
               jnp.repeat(d_skip.astype(F32), SSD_HEAD_DIM).reshape(1, -1),
               ssd_norm_w.reshape(1, -1).astype(F32), _expand_matrix())

    y = _out_ffn(x2, att.reshape(m, -1), ssd.reshape(m, -1),
                 w_out[:ATT_WIDTH].astype(BF16), w_out[ATT_WIDTH:].astype(BF16),
                 norm_mlp_w.reshape(1, d), w_ff1.astype(BF16), w_ff2.astype(BF16),
                 out_norm_w.reshape(1, d))
    return y.reshape(b, s, d)


def kernel(x, norm_mix_w, w_in, rel_bias, conv_w, conv_b, dt_bias, a_log, d_skip,
           ssd_norm_w, w_out, norm_mlp_w, w_ff1, w_ff2, norm_final_w):
    depth = w_in.shape[0]
    assert depth == 1, "final norm is fused into the single layer's last kernel"
    return _layer(x, norm_mix_w[0], w_in[0], rel_bias[0], conv_w[0], conv_b[0],
                  dt_bias[0], a_log[0], d_skip[0], ssd_norm_w[0], w_out[0],
                  norm_mlp_w[0], w_ff1[0], w_ff2[0], norm_final_w)
```

```python
import functools

import jax
import jax.numpy as jnp
from jax import lax
from jax.experimental import pallas as pl
from jax.experimental.pallas import tpu as pltpu

F32 = jnp.float32
BF16 = jnp.bfloat16

D_MODEL = 1024
CHUNK = 64
ATT_HEADS = 8
ATT_HEAD_DIM = 64
ATT_WIDTH = ATT_HEADS * ATT_HEAD_DIM
N_PREV_CHUNKS = 8
REL_CLIP = 256
SSD_HEAD_DIM = 64
SSD_WIDTH = D_MODEL
SSD_HEADS = SSD_WIDTH // SSD_HEAD_DIM
SSD_GROUPS = 2
SSD_STATE = 128
SSD_CONV = 4
CONV_DIM = SSD_WIDTH + 2 * SSD_GROUPS * SSD_STATE
MAIN_PROJ = 3 * ATT_WIDTH + SSD_WIDTH + CONV_DIM
D_FF = 4 * D_MODEL
EPS = 1e-5
NEG = -1e30

LANES = 128
SUBLANES = 8

PROJ_TM = 512
ATT_TQ = 256
ATT_BAND = ATT_TQ + N_PREV_CHUNKS * CHUNK
SSD_L = 128
SSD_T = 256
FFN_TM = 512
VMEM_LIMIT = 56 * 1024 * 1024


def _rms(x, w):
    ms = jnp.mean(x * x, axis=-1, keepdims=True)
    return x * lax.rsqrt(ms + EPS) * w


def _in_proj_kernel(x_ref, nw_ref, wm_ref, wkt_ref, wdt_ref,
                    q_ref, kt_ref, v_ref, z_ref, xbc_ref, dt_ref):
    hn = _rms(x_ref[...], nw_ref[...]).astype(BF16)

    def seg(lo, hi):
        return jnp.dot(hn, wm_ref[:, lo:hi], preferred_element_type=F32).astype(BF16)

    a = ATT_WIDTH
    q_ref[...] = seg(0, a)
    kt_ref[...] = lax.dot_general(wkt_ref[...], hn, (((1,), (1,)), ((), ())),
                                  preferred_element_type=F32).astype(BF16)
    v_ref[...] = seg(a, 2 * a)
    z_ref[...] = seg(2 * a, 2 * a + SSD_WIDTH)
    xbc_ref[...] = seg(2 * a + SSD_WIDTH, MAIN_PROJ - a)
    dt_ref[...] = jnp.dot(hn, wdt_ref[...], preferred_element_type=F32)


def _in_proj(x2, norm_w, w_main, w_kt, w_dt):
    m = x2.shape[0]
    tm = PROJ_TM
    row = lambda i: (i, 0)
    const = lambda i: (0, 0)
    out_shape = (
        jax.ShapeDtypeStruct((m, ATT_WIDTH), BF16),
        jax.ShapeDtypeStruct((ATT_WIDTH, m), BF16),
        jax.ShapeDtypeStruct((m, ATT_WIDTH), BF16),
        jax.ShapeDtypeStruct((m, SSD_WIDTH), BF16),
        jax.ShapeDtypeStruct((m, CONV_DIM), BF16),
        jax.ShapeDtypeStruct((m, LANES), F32),
    )
    return pl.pallas_call(
        _in_proj_kernel,
        out_shape=out_shape,
        grid=(m // tm,),
        in_specs=[
            pl.BlockSpec((tm, D_MODEL), row),
            pl.BlockSpec((1, D_MODEL), const),
            pl.BlockSpec((D_MODEL, MAIN_PROJ - ATT_WIDTH), const),
            pl.BlockSpec((ATT_WIDTH, D_MODEL), const),
            pl.BlockSpec((D_MODEL, LANES), const),
        ],
        out_specs=(
            pl.BlockSpec((tm, ATT_WIDTH), row),
            pl.BlockSpec((ATT_WIDTH, tm), lambda i: (0, i)),
            pl.BlockSpec((tm, ATT_WIDTH), row),
            pl.BlockSpec((tm, SSD_WIDTH), row),
            pl.BlockSpec((tm, CONV_DIM), row),
            pl.BlockSpec((tm, LANES), row),
        ),
        compiler_params=pltpu.CompilerParams(
            dimension_semantics=("parallel",), vmem_limit_bytes=VMEM_LIMIT),
        name="in_proj",
    )(x2, norm_w, w_main, w_kt, w_dt)


ATT_SLOTS = ATT_BAND // ATT_TQ
PAIR = 2 * ATT_HEAD_DIM
LOG2E = 1.4426950408889634


def _bias_table_kernel(r_ref, o_ref):
    x = jnp.broadcast_to(r_ref[0], (ATT_TQ, ATT_BAND + ATT_TQ))
    rolled = pltpu.roll(x, 0, 1, stride=1, stride_axis=0)
    t = rolled[:, ATT_TQ:]
    qc = lax.broadcasted_iota(jnp.int32, (ATT_TQ, ATT_BAND), 0) // CHUNK
    kc = lax.broadcasted_iota(jnp.int32, (ATT_TQ, ATT_BAND), 1) // CHUNK
    valid = (kc >= qc) & (kc <= qc + N_PREV_CHUNKS)
    o_ref[0] = jnp.where(valid, t * LOG2E, NEG)


def _bias_table(rel_bias):
    rb = rel_bias.astype(F32).T
    n = ATT_BAND + ATT_TQ
    far = jnp.broadcast_to(rb[:, 2 * REL_CLIP:], (ATT_HEADS, n - 2 * REL_CLIP + 1))
    near = rb[:, 2 * REL_CLIP - 1:0:-1]
    r = jnp.concatenate([far, near], axis=1).reshape(ATT_HEADS, 1, n)
    return pl.pallas_call(
        _bias_table_kernel,
        out_shape=jax.ShapeDtypeStruct((ATT_HEADS, ATT_TQ, ATT_BAND), F32),
        grid=(ATT_HEADS,),
        in_specs=[pl.BlockSpec((1, 1, n), lambda h: (h, 0, 0))],
        out_specs=pl.BlockSpec((1, ATT_TQ, ATT_BAND), lambda h: (h, 0, 0)),
        compiler_params=pltpu.CompilerParams(dimension_semantics=("parallel",)),
        name="bias_table",
    )(r)


def _attn_kernel(q_ref, kt_ref, v_ref, tab_ref, o_ref, kring, vring, tabs):
    i = pl.program_id(1)
    tq = ATT_TQ
    lane = lax.broadcasted_iota(jnp.int32, (tq, PAIR), 1)
    low = lane < ATT_HEAD_DIM

    @pl.when(i == 0)
    def _():
        kring[...] = jnp.zeros(kring.shape, BF16)
        vring[...] = jnp.zeros(vring.shape, BF16)

    @pl.when(i < ATT_SLOTS)
    def _():
        for m in range(ATT_SLOTS):
            off = jnp.where(i - (ATT_SLOTS - 1) + m < 0, NEG, 0.0).astype(F32)
            for h in range(ATT_HEADS):
                tabs[h, :, m * tq:(m + 1) * tq] = tab_ref[h, :, m * tq:(m + 1) * tq] + off

    slot_new = lax.rem(i, ATT_SLOTS)
    hd = ATT_HEAD_DIM
    for h in range(ATT_HEADS):
        r0 = h * PAIR + (h % 2) * hd
        kring[slot_new, r0:r0 + hd, :] = kt_ref[h * hd:(h + 1) * hd, :]
    for p in range(ATT_HEADS // 2):
        vp = v_ref[0, :, p * PAIR:(p + 1) * PAIR]
        one = jnp.ones_like(vp)
        vring[slot_new, :, (2 * p) * PAIR:(2 * p + 1) * PAIR] = jnp.where(low, vp, one)
        vring[slot_new, :, (2 * p + 1) * PAIR:(2 * p + 2) * PAIR] = jnp.where(low, one, vp)

    slots = [lax.rem(i + 1 + m, ATT_SLOTS) for m in range(ATT_SLOTS)]

    def scores(h):
        p = h // 2
        qp = q_ref[0, :, p * PAIR:(p + 1) * PAIR]
        return [jnp.dot(qp, kring[slots[m], h * PAIR:(h + 1) * PAIR, :],
                        preferred_element_type=F32)
                + tabs[h, :, m * tq:(m + 1) * tq] for m in range(ATT_SLOTS)]

    s_next = scores(0)
    res = []
    for h in range(ATT_HEADS):
        s = s_next
        if h + 1 < ATT_HEADS:
            s_next = scores(h + 1)
        mx = functools.reduce(jnp.maximum, s)
        mx = jnp.max(mx, axis=-1, keepdims=True)
        acc = None
        for m in range(ATT_SLOTS):
            e = jnp.exp2(s[m] - mx).astype(BF16)
            pv = jnp.dot(e, vring[slots[m], :, h * PAIR:(h + 1) * PAIR],
                         preferred_element_type=F32)
            acc = pv if acc is None else acc + pv
        res.append(acc)
        if h % 2 == 1:
            p = h // 2
            num = jnp.where(low, res[0], res[1])
            den = jnp.where(low, pltpu.roll(res[0], ATT_HEAD_DIM, 1),
                            pltpu.roll(res[1], ATT_HEAD_DIM, 1))
            o_ref[0, :, p * PAIR:(p + 1) * PAIR] = (num / den).astype(BF16)
            res = []


def _attention(q, kt, v, table):
    b, s, _ = q.shape
    tq = ATT_TQ
    nblk = s // tq
    blk = lambda bi, i: (bi, i, 0)
    return pl.pallas_call(
        _attn_kernel,
        out_shape=jax.ShapeDtypeStruct((b, s, ATT_WIDTH), BF16),
        grid=(b, s // tq),
        in_specs=[
            pl.BlockSpec((1, tq, ATT_WIDTH), blk),
            pl.BlockSpec((ATT_WIDTH, tq), lambda bi, i: (0, bi * nblk + i)),
            pl.BlockSpec((1, tq, ATT_WIDTH), blk),
            pl.BlockSpec((ATT_HEADS, tq, ATT_BAND), lambda bi, i: (0, 0, 0),
                         pipeline_mode=pl.Buffered(1)),
        ],
        out_specs=pl.BlockSpec((1, tq, ATT_WIDTH), blk),
        scratch_shapes=[
            pltpu.VMEM((ATT_SLOTS, ATT_HEADS * PAIR, tq), BF16),
            pltpu.VMEM((ATT_SLOTS, tq, ATT_HEADS * PAIR), BF16),
            pltpu.VMEM((ATT_HEADS, tq, ATT_BAND), F32),
        ],
        compiler_params=pltpu.CompilerParams(
            dimension_semantics=("parallel", "arbitrary"),
            vmem_limit_bytes=VMEM_LIMIT),
        name="band_attn",
    )(q, kt, v, table)


def _split3(v):
    v1 = v.astype(BF16).astype(F32)
    r1 = v - v1
    v2 = r1.astype(BF16).astype(F32)
    v3 = (r1 - v2).astype(BF16).astype(F32)
    return v1, v2, v3


def _expand_heads(v, e3_ref):
    lane = lax.broadcasted_iota(jnp.int32, v.shape, 1)
    v = jnp.where(lane < SSD_HEADS, v, 0.0)
    v1, v2, v3 = _split3(v)
    packed = v1 + pltpu.roll(v2, SSD_HEADS, 1) + pltpu.roll(v3, 2 * SSD_HEADS, 1)
    return jnp.dot(packed.astype(BF16), e3_ref[...], preferred_element_type=F32)


def _silu(x):
    h = 0.5 * x
    return h + h * jnp.tanh(h)


def _ssd_front(c, dt_ref, cw_ref, cb_ref, dtb_ref, alog_ref, e3_ref, tril3_ref,
               shift_ref, ubuf):
    L = SSD_L
    rows = slice(c * L, (c + 1) * L)

    win = ubuf[c * L:c * L + 2 * L, :]
    sh = jnp.dot(shift_ref[...], win, preferred_element_type=F32)
    acc = cb_ref[...] + cw_ref[SSD_CONV - 1:SSD_CONV, :] * win[L:, :].astype(F32)
    for d in range(1, SSD_CONV):
        acc = acc + cw_ref[SSD_CONV - 1 - d:SSD_CONV - d, :] * sh[(d - 1) * L:d * L, :]
    xc = _silu(acc)
    xs = xc[:, :SSD_WIDTH]
    xs_bf = xs.astype(BF16)
    bm = xc[:, SSD_WIDTH:SSD_WIDTH + SSD_GROUPS * SSD_STATE]
    cm = xc[:, SSD_WIDTH + SSD_GROUPS * SSD_STATE:]

    u = dt_ref[0, rows, :] + dtb_ref[...]
    dt = jnp.maximum(u, 0.0) + jnp.log1p(jnp.exp(-jnp.abs(u)))
    a = dt * (-jnp.exp(alog_ref[...]) * LOG2E)
    a1, a2, a3 = _split3(a)
    a_parts = jnp.concatenate([a1, a2, a3], axis=0).astype(BF16)
    a_cum = jnp.dot(tril3_ref[...], a_parts, preferred_element_type=F32)
    a_cum_t = a_cum.T
    dt_t = dt.T
    exp_acum = jnp.exp2(a_cum)
    a_last = a_cum[L - 1:L, :]
    w = dt * jnp.exp2(a_last - a_cum)
    xd = (xs * _expand_heads(w, e3_ref)).astype(BF16)
    cd = _expand_heads(jnp.broadcast_to(jnp.exp2(a_last), (SUBLANES, LANES)), e3_ref)[0:1, :]
    return xs, xs_bf, bm, cm, a_cum, a_cum_t, dt_t, exp_acum, xd, cd


def _ssd_back(c, front, z_ref, dexp_ref, nw_ref, y_ref, state, ybuf, causal, low):
    L = SSD_L
    gw = SSD_WIDTH // SSD_GROUPS
    ppg = SSD_HEADS // SSD_GROUPS // 2
    rows = slice(c * L, (c + 1) * L)
    xs, xs_bf, bm, cm, a_cum, a_cum_t, dt_t, exp_acum, xd, cd = front

    zero = jnp.zeros((L, PAIR), BF16)
    for g in range(SSD_GROUPS):
        bg = bm[:, g * SSD_STATE:(g + 1) * SSD_STATE]
        cg = cm[:, g * SSD_STATE:(g + 1) * SSD_STATE]
        cb = lax.dot_general(cg.astype(BF16), bg.astype(BF16), (((1,), (1,)), ((), ())),
                             preferred_element_type=F32)
        sg = state[g]
        sg_bf = sg.astype(BF16)
        for q in range(ppg):
            p = g * ppg + q
            lhs = []
            for h in (2 * p, 2 * p + 1):
                seg = a_cum[:, h:h + 1] - a_cum_t[h:h + 1, :]
                dec = jnp.where(causal, jnp.exp2(seg), 0.0)
                lhs.append((cb * dec * dt_t[h:h + 1, :]).astype(BF16))
            for h in (2 * p, 2 * p + 1):
                lhs.append((cg * exp_acum[:, h:h + 1]).astype(BF16))
            xp = xs_bf[:, p * PAIR:(p + 1) * PAIR]
            sp = sg_bf[:, q * PAIR:(q + 1) * PAIR]
            rhs = jnp.concatenate([jnp.where(low, xp, zero), jnp.where(low, zero, xp),
                                   jnp.where(low, sp, zero), jnp.where(low, zero, sp)], axis=0)
            yp = jnp.dot(jnp.concatenate(lhs, axis=1), rhs, preferred_element_type=F32)
            yp = yp + dexp_ref[:, p * PAIR:(p + 1) * PAIR] * xs[:, p * PAIR:(p + 1) * PAIR]
            zf = z_ref[0, rows, p * PAIR:(p + 1) * PAIR].astype(F32)
            ybuf[:, p * PAIR:(p + 1) * PAIR] = yp * _silu(zf)
        s_new = jnp.dot(bg.T.astype(BF16), xd[:, g * gw:(g + 1) * gw],
                        preferred_element_type=F32)
        state[g] = sg * cd[:, g * gw:(g + 1) * gw] + s_new

    for g in range(SSD_GROUPS):
        yg = ybuf[:, g * gw:(g + 1) * gw]
        ms = jnp.mean(yg * yg, axis=-1, keepdims=True)
        y_ref[0, rows, g * gw:(g + 1) * gw] = (
            yg * lax.rsqrt(ms + EPS) * nw_ref[:, g * gw:(g + 1) * gw]).astype(BF16)


def _ssd_kernel(z_ref, xbc_ref, dt_ref, cw_ref, cb_ref, dtb_ref, alog_ref,
                dexp_ref, nw_ref, e3_ref, tril3_ref, shift_ref, y_ref, ubuf, state, ybuf):
    j = pl.program_id(1)
    T = SSD_T
    L = SSD_L

    @pl.when(j == 0)
    def _():
        ubuf[0:L, :] = jnp.zeros((L, CONV_DIM), BF16)
        state[...] = jnp.zeros(state.shape, F32)

    @pl.when(j > 0)
    def _():
        ubuf[0:L, :] = ubuf[T:T + L, :]

    ubuf[L:T + L, :] = xbc_ref[0]

    row = lax.broadcasted_iota(jnp.int32, (L, L), 0)
    col = lax.broadcasted_iota(jnp.int32, (L, L), 1)
    causal = row >= col
    low = lax.broadcasted_iota(jnp.int32, (L, PAIR), 1) < SSD_HEAD_DIM
    for c in range(T // L):
        front = _ssd_front(c, dt_ref, cw_ref, cb_ref, dtb_ref, alog_ref, e3_ref,
                           tril3_ref, shift_ref, ubuf)
        _ssd_back(c, front, z_ref, dexp_ref, nw_ref, y_ref, state, ybuf, causal, low)


def _ssd(z, xbc, dt_raw, conv_w, conv_b, dt_bias_p, a_log_p, d_exp, norm_w, e3, tril3, shift):
    b, s, _ = z.shape
    T = SSD_T
    blk = lambda bi, j: (bi, j, 0)
    const = lambda bi, j: (0, 0)
    once = pl.Buffered(1)
    return pl.pallas_call(
        _ssd_kernel,
        out_shape=jax.ShapeDtypeStruct((b, s, SSD_WIDTH), BF16),
        grid=(b, s // T),
        in_specs=[
            pl.BlockSpec((1, T, SSD_WIDTH), blk),
            pl.BlockSpec((1, T, CONV_DIM), blk),
            pl.BlockSpec((1, T, LANES), blk),
            pl.BlockSpec((SSD_CONV, CONV_DIM), const),
            pl.BlockSpec((1, CONV_DIM), const),
            pl.BlockSpec((1, LANES), const),
            pl.BlockSpec((1, LANES), const),
            pl.BlockSpec((1, SSD_WIDTH), const),
            pl.BlockSpec((1, SSD_WIDTH), const),
            pl.BlockSpec((LANES, SSD_WIDTH), const, pipeline_mode=once),
            pl.BlockSpec((SSD_L, 3 * SSD_L), const, pipeline_mode=once),
            pl.BlockSpec(((SSD_CONV - 1) * SSD_L, 2 * SSD_L), const, pipeline_mode=once),
        ],
        out_specs=pl.BlockSpec((1, T, SSD_WIDTH), blk),
        scratch_shapes=[
            pltpu.VMEM((T + SSD_L, CONV_DIM), BF16),
            pltpu.VMEM((SSD_GROUPS, SSD_STATE, SSD_WIDTH // SSD_GROUPS), F32),
            pltpu.VMEM((SSD_L, SSD_WIDTH), F32),
        ],
        compiler_params=pltpu.CompilerParams(
            dimension_semantics=("parallel", "arbitrary"),
            vmem_limit_bytes=VMEM_LIMIT),
        name="ssd_mixer",
    )(z, xbc, dt_raw, conv_w, conv_b, dt_bias_p, a_log_p, d_exp, norm_w, e3, tril3, shift)


def _shift_matrix():
    r = jnp.arange((SSD_CONV - 1) * SSD_L)[:, None]
    c = jnp.arange(2 * SSD_L)[None, :]
    return (c == SSD_L + r % SSD_L - (r // SSD_L + 1)).astype(BF16)


def _tril3_matrix():
    r = jnp.arange(SSD_L)[:, None]
    c = jnp.arange(3 * SSD_L)[None, :] % SSD_L
    return (r >= c).astype(BF16)


def _expand_matrix():
    r = jnp.arange(LANES)[:, None]
    c = jnp.arange(SSD_WIDTH)[None, :]
    hit = (r < 3 * SSD_HEADS) & ((r % SSD_HEADS) == c // SSD_HEAD_DIM)
    return hit.astype(BF16)


def _out_ffn_kernel(x_ref, att_ref, ssd_ref, woa_ref, wos_ref, nw_ref,
                    w1_ref, w2_ref, fw_ref, o_ref):
    h = (x_ref[...]
         + jnp.dot(att_ref[...], woa_ref[...], preferred_element_type=F32)
         + jnp.dot(ssd_ref[...], wos_ref[...], preferred_element_type=F32))
    hn = _rms(h, nw_ref[...]).astype(BF16)
    hid = jnp.dot(hn, w1_ref[...], preferred_element_type=F32)
    hid = jnp.square(jnp.maximum(hid, 0.0)).astype(BF16)
    h = h + jnp.dot(hid, w2_ref[...], preferred_element_type=F32)
    o_ref[...] = _rms(h, fw_ref[...])


def _out_ffn(x2, att2, ssd2, wo_a, wo_s, norm_w, w1, w2, final_w):
    m = x2.shape[0]
    tm = FFN_TM
    row = lambda i: (i, 0)
    const = lambda i: (0, 0)
    once = pl.Buffered(1)
    return pl.pallas_call(
        _out_ffn_kernel,
        out_shape=jax.ShapeDtypeStruct((m, D_MODEL), F32),
        grid=(m // tm,),
        in_specs=[
            pl.BlockSpec((tm, D_MODEL), row),
            pl.BlockSpec((tm, ATT_WIDTH), row),
            pl.BlockSpec((tm, SSD_WIDTH), row),
            pl.BlockSpec((ATT_WIDTH, D_MODEL), const, pipeline_mode=once),
            pl.BlockSpec((SSD_WIDTH, D_MODEL), const, pipeline_mode=once),
            pl.BlockSpec((1, D_MODEL), const),
            pl.BlockSpec((D_MODEL, D_FF), const, pipeline_mode=once),
            pl.BlockSpec((D_FF, D_MODEL), const, pipeline_mode=once),
            pl.BlockSpec((1, D_MODEL), const),
        ],
        out_specs=pl.BlockSpec((tm, D_MODEL), row),
        compiler_params=pltpu.CompilerParams(
            dimension_semantics=("parallel",), vmem_limit_bytes=VMEM_LIMIT),
        name="out_ffn",
    )(x2, att2, ssd2, wo_a, wo_s, norm_w, w1, w2, final_w)


def _layer(x, norm_mix_w, w_in, rel_bias, conv_w, conv_b, dt_bias, a_log, d_skip,
           ssd_norm_w, w_out, norm_mlp_w, w_ff1, w_ff2, out_norm_w):
    b, s, d = x.shape
    m = b * s
    x2 = x.reshape(m, d)

    scale = ATT_HEAD_DIM ** -0.5 * LOG2E
    w_main = jnp.concatenate(
        [w_in[:, :ATT_WIDTH] * scale, w_in[:, 2 * ATT_WIDTH:MAIN_PROJ]], axis=1).astype(BF16)
    w_kt = w_in[:, ATT_WIDTH:2 * ATT_WIDTH].T.astype(BF16)
    w_dt = jnp.pad(w_in[:, MAIN_PROJ:], ((0, 0), (0, LANES - SSD_HEADS))).astype(BF16)
    pad_h = lambda v: jnp.pad(v.astype(F32), (0, LANES - SSD_HEADS)).reshape(1, LANES)

    q, kt, v, z, xbc, dt_raw = _in_proj(x2, norm_mix_w.reshape(1, d), w_main, w_kt, w_dt)

    att = _attention(q.reshape(b, s, -1), kt, v.reshape(b, s, -1), _bias_table(rel_bias))
    ssd = _ssd(z.reshape(b, s, -1), xbc.reshape(b, s, -1), dt_raw.reshape(b, s, -1),
               conv_w.astype(F32), conv_b.reshape(1, -1).astype(F32),
               pad_h(dt_bias), pad_h(a_log),
               jnp.repeat(d_skip.astype(F32), SSD_HEAD_DIM).reshape(1, -1),
               ssd_norm_w.reshape(1, -1).astype(F32), _expand_matrix(), _tril3_matrix(), _shift_matrix())

    y = _out_ffn(x2, att.reshape(m, -1), ssd.reshape(m, -1),
                 w_out[:ATT_WIDTH].astype(BF16), w_out[ATT_WIDTH:].astype(BF16),
                 norm_mlp_w.reshape(1, d), w_ff1.astype(BF16), w_ff2.astype(BF16),
                 out_norm_w.reshape(1, d))
    return y.reshape(b, s, d)


def kernel(x, norm_mix_w, w_in, rel_bias, conv_w, conv_b, dt_bias, a_log, d_skip,
           ssd_norm_w, w_out, norm_mlp_w, w_ff1, w_ff2, norm_final_w):
    depth = w_in.shape[0]
    assert depth == 1, "final norm is fused into the single layer's last kernel"
    return _layer(x, norm_mix_w[0], w_in[0], rel_bias[0], conv_w[0], conv_b[0],
                  dt_bias[0], a_log[0], d_skip[0], ssd_norm_w[0], w_out[0],
                  norm_mlp_w[0], w_ff1[0], w_ff2[0], norm_final_w)
```

```python
import functools

import jax
import jax.numpy as jnp
from jax import lax
from jax.experimental import pallas as pl
from jax.experimental.pallas import tpu as pltpu

F32 = jnp.float32
BF16 = jnp.bfloat16

D_MODEL = 1024
CHUNK = 64
ATT_HEADS = 8
ATT_HEAD_DIM = 64
ATT_WIDTH = ATT_HEADS * ATT_HEAD_DIM
N_PREV_CHUNKS = 8
REL_CLIP = 256
SSD_HEAD_DIM = 64
SSD_WIDTH = D_MODEL
SSD_HEADS = SSD_WIDTH // SSD_HEAD_DIM
SSD_GROUPS = 2
SSD_STATE = 128
SSD_CONV = 4
CONV_DIM = SSD_WIDTH + 2 * SSD_GROUPS * SSD_STATE
MAIN_PROJ = 3 * ATT_WIDTH + SSD_WIDTH + CONV_DIM
D_FF = 4 * D_MODEL
EPS = 1e-5
NEG = -1e30

LANES = 128
SUBLANES = 8

PROJ_TM = 1024
ATT_TQ = 256
ATT_BAND = ATT_TQ + N_PREV_CHUNKS * CHUNK
SSD_L = 128
SSD_T = 512
FFN_TM = 512
VMEM_LIMIT = 56 * 1024 * 1024


def _rms(x, w):
    ms = jnp.mean(x * x, axis=-1, keepdims=True)
    return x * lax.rsqrt(ms + EPS) * w


def _in_proj_kernel(x_ref, nw_ref, wm_ref, wkt_ref,
                    q_ref, kt_ref, v_ref, z_ref, xbc_ref, dtt_ref):
    hn = _rms(x_ref[...], nw_ref[...]).astype(BF16)

    def seg(lo, hi):
        return jnp.dot(hn, wm_ref[:, lo:hi], preferred_element_type=F32).astype(BF16)

    a = ATT_WIDTH
    q_ref[...] = seg(0, a)
    ktd = lax.dot_general(wkt_ref[...], hn, (((1,), (1,)), ((), ())),
                          preferred_element_type=F32)
    kt_ref[...] = ktd[:a, :].astype(BF16)
    dtt_ref[...] = ktd[a:, :]
    v_ref[...] = seg(a, 2 * a)
    z_ref[...] = seg(2 * a, 2 * a + SSD_WIDTH)
    xbc_ref[...] = seg(2 * a + SSD_WIDTH, MAIN_PROJ - a)


def _in_proj(x2, norm_w, w_main, w_ktd):
    m = x2.shape[0]
    tm = PROJ_TM
    row = lambda i: (i, 0)
    col = lambda i: (0, i)
    const = lambda i: (0, 0)
    once = pl.Buffered(1)
    out_shape = (
        jax.ShapeDtypeStruct((m, ATT_WIDTH), BF16),
        jax.ShapeDtypeStruct((ATT_WIDTH, m), BF16),
        jax.ShapeDtypeStruct((m, ATT_WIDTH), BF16),
        jax.ShapeDtypeStruct((m, SSD_WIDTH), BF16),
        jax.ShapeDtypeStruct((m, CONV_DIM), BF16),
        jax.ShapeDtypeStruct((SSD_HEADS, m), F32),
    )
    return pl.pallas_call(
        _in_proj_kernel,
        out_shape=out_shape,
        grid=(m // tm,),
        in_specs=[
            pl.BlockSpec((tm, D_MODEL), row),
            pl.BlockSpec((1, D_MODEL), const),
            pl.BlockSpec((D_MODEL, MAIN_PROJ - ATT_WIDTH), const, pipeline_mode=once),
            pl.BlockSpec((ATT_WIDTH + SSD_HEADS, D_MODEL), const, pipeline_mode=once),
        ],
        out_specs=(
            pl.BlockSpec((tm, ATT_WIDTH), row),
            pl.BlockSpec((ATT_WIDTH, tm), col),
            pl.BlockSpec((tm, ATT_WIDTH), row),
            pl.BlockSpec((tm, SSD_WIDTH), row),
            pl.BlockSpec((tm, CONV_DIM), row),
            pl.BlockSpec((SSD_HEADS, tm), col),
        ),
        compiler_params=pltpu.CompilerParams(
            dimension_semantics=("parallel",), vmem_limit_bytes=VMEM_LIMIT),
        name="in_proj",
    )(x2, norm_w, w_main, w_ktd)


ATT_SLOTS = ATT_BAND // ATT_TQ
PAIR = 2 * ATT_HEAD_DIM
LOG2E = 1.4426950408889634


def _bias_table_kernel(r_ref, o_ref):
    x = jnp.broadcast_to(r_ref[0], (ATT_TQ, ATT_BAND + ATT_TQ))
    rolled = pltpu.roll(x, 0, 1, stride=1, stride_axis=0)
    t = rolled[:, ATT_TQ:]
    qc = lax.broadcasted_iota(jnp.int32, (ATT_TQ, ATT_BAND), 0) // CHUNK
    kc = lax.broadcasted_iota(jnp.int32, (ATT_TQ, ATT_BAND), 1) // CHUNK
    valid = (kc >= qc) & (kc <= qc + N_PREV_CHUNKS)
    o_ref[0] = jnp.where(valid, t * LOG2E, NEG)


def _bias_table(rel_bias):
    rb = rel_bias.astype(F32).T
    n = ATT_BAND + ATT_TQ
    far = jnp.broadcast_to(rb[:, 2 * REL_CLIP:], (ATT_HEADS, n - 2 * REL_CLIP + 1))
    near = rb[:, 2 * REL_CLIP - 1:0:-1]
    r = jnp.concatenate([far, near], axis=1).reshape(ATT_HEADS, 1, n)
    return pl.pallas_call(
        _bias_table_kernel,
        out_shape=jax.ShapeDtypeStruct((ATT_HEADS, ATT_TQ, ATT_BAND), F32),
        grid=(ATT_HEADS,),
        in_specs=[pl.BlockSpec((1, 1, n), lambda h: (h, 0, 0))],
        out_specs=pl.BlockSpec((1, ATT_TQ, ATT_BAND), lambda h: (h, 0, 0)),
        compiler_params=pltpu.CompilerParams(dimension_semantics=("parallel",)),
        name="bias_table",
    )(r)


def _attn_kernel(q_ref, kt_ref, v_ref, tab_ref, o_ref, kring, vring, tabs):
    i = pl.program_id(1)
    tq = ATT_TQ
    lane = lax.broadcasted_iota(jnp.int32, (tq, PAIR), 1)
    low = lane < ATT_HEAD_DIM

    @pl.when(i == 0)
    def _():
        kring[...] = jnp.zeros(kring.shape, BF16)
        vring[...] = jnp.zeros(vring.shape, BF16)

    @pl.when(i < ATT_SLOTS)
    def _():
        for m in range(ATT_SLOTS):
            off = jnp.where(i - (ATT_SLOTS - 1) + m < 0, NEG, 0.0).astype(F32)
            for h in range(ATT_HEADS):
                tabs[h, :, m * tq:(m + 1) * tq] = tab_ref[h, :, m * tq:(m + 1) * tq] + off

    slot_new = lax.rem(i, ATT_SLOTS)
    hd = ATT_HEAD_DIM
    for h in range(ATT_HEADS):
        r0 = h * PAIR + (h % 2) * hd
        kring[slot_new, r0:r0 + hd, :] = kt_ref[h * hd:(h + 1) * hd, :]
    for p in range(ATT_HEADS // 2):
        vp = v_ref[0, :, p * PAIR:(p + 1) * PAIR]
        one = jnp.ones_like(vp)
        vring[slot_new, :, (2 * p) * PAIR:(2 * p + 1) * PAIR] = jnp.where(low, vp, one)
        vring[slot_new, :, (2 * p + 1) * PAIR:(2 * p + 2) * PAIR] = jnp.where(low, one, vp)

    slots = [lax.rem(i + 1 + m, ATT_SLOTS) for m in range(ATT_SLOTS)]

    def scores(h):
        p = h // 2
        qp = q_ref[0, :, p * PAIR:(p + 1) * PAIR]
        return [jnp.dot(qp, kring[slots[m], h * PAIR:(h + 1) * PAIR, :],
                        preferred_element_type=F32)
                + tabs[h, :, m * tq:(m + 1) * tq] for m in range(ATT_SLOTS)]

    s_next = scores(0)
    res = []
    for h in range(ATT_HEADS):
        s = s_next
        if h + 1 < ATT_HEADS:
            s_next = scores(h + 1)
        mx = functools.reduce(jnp.maximum, s)
        mx = jnp.max(mx, axis=-1, keepdims=True)
        acc = None
        for m in range(ATT_SLOTS):
            e = jnp.exp2(s[m] - mx).astype(BF16)
            pv = jnp.dot(e, vring[slots[m], :, h * PAIR:(h + 1) * PAIR],
                         preferred_element_type=F32)
            acc = pv if acc is None else acc + pv
        res.append(acc)
        if h % 2 == 1:
            p = h // 2
            num = jnp.where(low, res[0], res[1])
            den = jnp.where(low, pltpu.roll(res[0], ATT_HEAD_DIM, 1),
                            pltpu.roll(res[1], ATT_HEAD_DIM, 1))
            o_ref[0, :, p * PAIR:(p + 1) * PAIR] = (num / den).astype(BF16)
            res = []


def _attention(q, kt, v, table):
    b, s, _ = q.shape
    tq = ATT_TQ
    nblk = s // tq
    blk = lambda bi, i: (bi, i, 0)
    return pl.pallas_call(
        _attn_kernel,
        out_shape=jax.ShapeDtypeStruct((b, s, ATT_WIDTH), BF16),
        grid=(b, s // tq),
        in_specs=[
            pl.BlockSpec((1, tq, ATT_WIDTH), blk),
            pl.BlockSpec((ATT_WIDTH, tq), lambda bi, i: (0, bi * nblk + i)),
            pl.BlockSpec((1, tq, ATT_WIDTH), blk),
            pl.BlockSpec((ATT_HEADS, tq, ATT_BAND), lambda bi, i: (0, 0, 0),
                         pipeline_mode=pl.Buffered(1)),
        ],
        out_specs=pl.BlockSpec((1, tq, ATT_WIDTH), blk),
        scratch_shapes=[
            pltpu.VMEM((ATT_SLOTS, ATT_HEADS * PAIR, tq), BF16),
            pltpu.VMEM((ATT_SLOTS, tq, ATT_HEADS * PAIR), BF16),
            pltpu.VMEM((ATT_HEADS, tq, ATT_BAND), F32),
        ],
        compiler_params=pltpu.CompilerParams(
            dimension_semantics=("parallel", "arbitrary"),
            vmem_limit_bytes=VMEM_LIMIT),
        name="band_attn",
    )(q, kt, v, table)


def _split3(v):
    v1 = v.astype(BF16).astype(F32)
    r1 = v - v1
    v2 = r1.astype(BF16).astype(F32)
    v3 = (r1 - v2).astype(BF16).astype(F32)
    return v1, v2, v3


def _expand_heads(v, e3_ref):
    lane = lax.broadcasted_iota(jnp.int32, v.shape, 1)
    v = jnp.where(lane < SSD_HEADS, v, 0.0)
    v1, v2, v3 = _split3(v)
    packed = v1 + pltpu.roll(v2, SSD_HEADS, 1) + pltpu.roll(v3, 2 * SSD_HEADS, 1)
    return jnp.dot(packed.astype(BF16), e3_ref[...], preferred_element_type=F32)


def _silu(x):
    h = 0.5 * x
    return h + h * jnp.tanh(h)


def _ssd_front(c, dtt_ref, cw_ref, cb_ref, dtb_ref, alog_ref, e3_ref, tril3_ref,
               shift_ref, ubuf):
    L = SSD_L
    rows = slice(c * L, (c + 1) * L)

    win = ubuf[c * L:c * L + 2 * L, :]
    sh = jnp.dot(shift_ref[...], win, preferred_element_type=F32)
    acc = cb_ref[...] + cw_ref[SSD_CONV - 1:SSD_CONV, :] * win[L:, :].astype(F32)
    for d in range(1, SSD_CONV):
        acc = acc + cw_ref[SSD_CONV - 1 - d:SSD_CONV - d, :] * sh[(d - 1) * L:d * L, :]
    xc = _silu(acc)
    xs = xc[:, :SSD_WIDTH]
    xs_bf = xs.astype(BF16)
    bm = xc[:, SSD_WIDTH:SSD_WIDTH + SSD_GROUPS * SSD_STATE]
    cm = xc[:, SSD_WIDTH + SSD_GROUPS * SSD_STATE:]

    raw_t = jnp.concatenate([dtt_ref[:, rows], jnp.zeros((LANES - SSD_HEADS, L), F32)], axis=0)
    u = raw_t.T + dtb_ref[...]
    dt = jnp.maximum(u, 0.0) + jnp.log1p(jnp.exp(-jnp.abs(u)))
    a = dt * (-jnp.exp(alog_ref[...]) * LOG2E)
    a1, a2, a3 = _split3(a)
    a_parts = jnp.concatenate([a1, a2, a3], axis=0).astype(BF16)
    a_cum = jnp.dot(tril3_ref[...], a_parts, preferred_element_type=F32)
    a_cum_t = a_cum.T
    dt_t = dt.T
    exp_acum = jnp.exp2(a_cum)
    a_last = a_cum[L - 1:L, :]
    w = dt * jnp.exp2(a_last - a_cum)
    xd = (xs * _expand_heads(w, e3_ref)).astype(BF16)
    cd = _expand_heads(jnp.broadcast_to(jnp.exp2(a_last), (SUBLANES, LANES)), e3_ref)[0:1, :]
    return xs, xs_bf, bm, cm, a_cum, a_cum_t, dt_t, exp_acum, xd, cd


def _ssd_back(c, front, z_ref, dexp_ref, nw_ref, y_ref, state, ybuf, causal, low):
    L = SSD_L
    gw = SSD_WIDTH // SSD_GROUPS
    ppg = SSD_HEADS // SSD_GROUPS // 2
    rows = slice(c * L, (c + 1) * L)
    xs, xs_bf, bm, cm, a_cum, a_cum_t, dt_t, exp_acum, xd, cd = front

    zero = jnp.zeros((L, PAIR), BF16)
    for g in range(SSD_GROUPS):
        bg = bm[:, g * SSD_STATE:(g + 1) * SSD_STATE]
        cg = cm[:, g * SSD_STATE:(g + 1) * SSD_STATE]
        cb = lax.dot_general(cg.astype(BF16), bg.astype(BF16), (((1,), (1,)), ((), ())),
                             preferred_element_type=F32)
        sg = state[g]
        sg_bf = sg.astype(BF16)
        for q in range(ppg):
            p = g * ppg + q
            lhs = []
            for h in (2 * p, 2 * p + 1):
                seg = a_cum[:, h:h + 1] - a_cum_t[h:h + 1, :]
                dec = jnp.where(causal, jnp.exp2(seg), 0.0)
                lhs.append((cb * dec * dt_t[h:h + 1, :]).astype(BF16))
            for h in (2 * p, 2 * p + 1):
                lhs.append((cg * exp_acum[:, h:h + 1]).astype(BF16))
            xp = xs_bf[:, p * PAIR:(p + 1) * PAIR]
            sp = sg_bf[:, q * PAIR:(q + 1) * PAIR]
            rhs = jnp.concatenate([jnp.where(low, xp, zero), jnp.where(low, zero, xp),
                                   jnp.where(low, sp, zero), jnp.where(low, zero, sp)], axis=0)
            yp = jnp.dot(jnp.concatenate(lhs, axis=1), rhs, preferred_element_type=F32)
            yp = yp + dexp_ref[:, p * PAIR:(p + 1) * PAIR] * xs[:, p * PAIR:(p + 1) * PAIR]
            zf = z_ref[0, rows, p * PAIR:(p + 1) * PAIR].astype(F32)
            ybuf[:, p * PAIR:(p + 1) * PAIR] = yp * _silu(zf)
        s_new = jnp.dot(bg.T.astype(BF16), xd[:, g * gw:(g + 1) * gw],
                        preferred_element_type=F32)
        state[g] = sg * cd[:, g * gw:(g + 1) * gw] + s_new

    for g in range(SSD_GROUPS):
        yg = ybuf[:, g * gw:(g + 1) * gw]
        ms = jnp.mean(yg * yg, axis=-1, keepdims=True)
        y_ref[0, rows, g * gw:(g + 1) * gw] = (
            yg * lax.rsqrt(ms + EPS) * nw_ref[:, g * gw:(g + 1) * gw]).astype(BF16)


def _ssd_kernel(z_ref, xbc_ref, dtt_ref, cw_ref, cb_ref, dtb_ref, alog_ref,
                dexp_ref, nw_ref, e3_ref, tril3_ref, shift_ref, y_ref, ubuf, state, ybuf):
    j = pl.program_id(1)
    T = SSD_T
    L = SSD_L

    @pl.when(j == 0)
    def _():
        ubuf[0:L, :] = jnp.zeros((L, CONV_DIM), BF16)
        state[...] = jnp.zeros(state.shape, F32)

    @pl.when(j > 0)
    def _():
        ubuf[0:L, :] = ubuf[T:T + L, :]

    ubuf[L:T + L, :] = xbc_ref[0]

    row = lax.broadcasted_iota(jnp.int32, (L, L), 0)
    col = lax.broadcasted_iota(jnp.int32, (L, L), 1)
    causal = row >= col
    low = lax.broadcasted_iota(jnp.int32, (L, PAIR), 1) < SSD_HEAD_DIM
    for c in range(T // L):
        front = _ssd_front(c, dtt_ref, cw_ref, cb_ref, dtb_ref, alog_ref, e3_ref,
                           tril3_ref, shift_ref, ubuf)
        _ssd_back(c, front, z_ref, dexp_ref, nw_ref, y_ref, state, ybuf, causal, low)


def _ssd(z, xbc, dtt_raw, conv_w, conv_b, dt_bias_p, a_log_p, d_exp, norm_w, e3, tril3, shift):
    b, s, _ = z.shape
    T = SSD_T
    nblk = s // T
    blk = lambda bi, j: (bi, j, 0)
    const = lambda bi, j: (0, 0)
    once = pl.Buffered(1)
    return pl.pallas_call(
        _ssd_kernel,
        out_shape=jax.ShapeDtypeStruct((b, s, SSD_WIDTH), BF16),
        grid=(b, s // T),
        in_specs=[
            pl.BlockSpec((1, T, SSD_WIDTH), blk),
            pl.BlockSpec((1, T, CONV_DIM), blk),
            pl.BlockSpec((SSD_HEADS, T), lambda bi, j: (0, bi * nblk + j)),
            pl.BlockSpec((SSD_CONV, CONV_DIM), const),
            pl.BlockSpec((1, CONV_DIM), const),
            pl.BlockSpec((1, LANES), const),
            pl.BlockSpec((1, LANES), const),
            pl.BlockSpec((1, SSD_WIDTH), const),
            pl.BlockSpec((1, SSD_WIDTH), const),
            pl.BlockSpec((LANES, SSD_WIDTH), const, pipeline_mode=once),
            pl.BlockSpec((SSD_L, 3 * SSD_L), const, pipeline_mode=once),
            pl.BlockSpec(((SSD_CONV - 1) * SSD_L, 2 * SSD_L), const, pipeline_mode=once),
        ],
        out_specs=pl.BlockSpec((1, T, SSD_WIDTH), blk),
        scratch_shapes=[
            pltpu.VMEM((T + SSD_L, CONV_DIM), BF16),
            pltpu.VMEM((SSD_GROUPS, SSD_STATE, SSD_WIDTH // SSD_GROUPS), F32),
            pltpu.VMEM((SSD_L, SSD_WIDTH), F32),
        ],
        compiler_params=pltpu.CompilerParams(
            dimension_semantics=("parallel", "arbitrary"),
            vmem_limit_bytes=VMEM_LIMIT),
        name="ssd_mixer",
    )(z, xbc, dtt_raw, conv_w, conv_b, dt_bias_p, a_log_p, d_exp, norm_w, e3, tril3, shift)


def _shift_matrix():
    r = jnp.arange((SSD_CONV - 1) * SSD_L)[:, None]
    c = jnp.arange(2 * SSD_L)[None, :]
    return (c == SSD_L + r % SSD_L - (r // SSD_L + 1)).astype(BF16)


def _tril3_matrix():
    r = jnp.arange(SSD_L)[:, None]
    c = jnp.arange(3 * SSD_L)[None, :] % SSD_L
    return (r >= c).astype(BF16)


def _expand_matrix():
    r = jnp.arange(LANES)[:, None]
    c = jnp.arange(SSD_WIDTH)[None, :]
    hit = (r < 3 * SSD_HEADS) & ((r % SSD_HEADS) == c // SSD_HEAD_DIM)
    return hit.astype(BF16)


def _out_ffn_kernel(x_ref, att_ref, ssd_ref, woa_ref, wos_ref, nw_ref,
                    w1_ref, w2_ref, fw_ref, o_ref):
    h = (x_ref[...]
         + jnp.dot(att_ref[...], woa_ref[...], preferred_element_type=F32)
         + jnp.dot(ssd_ref[...], wos_ref[...], preferred_element_type=F32))
    hn = _rms(h, nw_ref[...]).astype(BF16)
    hid = jnp.dot(hn, w1_ref[...], preferred_element_type=F32)
    hid = jnp.square(jnp.maximum(hid, 0.0)).astype(BF16)
    h = h + jnp.dot(hid, w2_ref[...], preferred_element_type=F32)
    o_ref[...] = _rms(h, fw_ref[...])


def _out_ffn(x2, att2, ssd2, wo_a, wo_s, norm_w, w1, w2, final_w):
    m = x2.shape[0]
    tm = FFN_TM
    row = lambda i: (i, 0)
    const = lambda i: (0, 0)
    once = pl.Buffered(1)
    return pl.pallas_call(
        _out_ffn_kernel,
        out_shape=jax.ShapeDtypeStruct((m, D_MODEL), F32),
        grid=(m // tm,),
        in_specs=[
            pl.BlockSpec((tm, D_MODEL), row),
            pl.BlockSpec((tm, ATT_WIDTH), row),
            pl.BlockSpec((tm, SSD_WIDTH), row),
            pl.BlockSpec((ATT_WIDTH, D_MODEL), const, pipeline_mode=once),
            pl.BlockSpec((SSD_WIDTH, D_MODEL), const, pipeline_mode=once),
            pl.BlockSpec((1, D_MODEL), const),
            pl.BlockSpec((D_MODEL, D_FF), const, pipeline_mode=once),
            pl.BlockSpec((D_FF, D_MODEL), const, pipeline_mode=once),
            pl.BlockSpec((1, D_MODEL), const),
        ],
        out_specs=pl.BlockSpec((tm, D_MODEL), row),
        compiler_params=pltpu.CompilerParams(
            dimension_semantics=("parallel",), vmem_limit_bytes=VMEM_LIMIT),
        name="out_ffn",
    )(x2, att2, ssd2, wo_a, wo_s, norm_w, w1, w2, final_w)


def _layer(x, norm_mix_w, w_in, rel_bias, conv_w, conv_b, dt_bias, a_log, d_skip,
           ssd_norm_w, w_out, norm_mlp_w, w_ff1, w_ff2, out_norm_w):
    b, s, d = x.shape
    m = b * s
    x2 = x.reshape(m, d)

    scale = ATT_HEAD_DIM ** -0.5 * LOG2E
    w_main = jnp.concatenate(
        [w_in[:, :ATT_WIDTH] * scale, w_in[:, 2 * ATT_WIDTH:MAIN_PROJ]], axis=1).astype(BF16)
    w_ktd = jnp.concatenate([w_in[:, ATT_WIDTH:2 * ATT_WIDTH], w_in[:, MAIN_PROJ:]],
                            axis=1).T.astype(BF16)
    pad_h = lambda v: jnp.pad(v.astype(F32), (0, LANES - SSD_HEADS)).reshape(1, LANES)

    q, kt, v, z, xbc, dtt_raw = _in_proj(x2, norm_mix_w.reshape(1, d), w_main, w_ktd)

    att = _attention(q.reshape(b, s, -1), kt, v.reshape(b, s, -1), _bias_table(rel_bias))
    ssd = _ssd(z.reshape(b, s, -1), xbc.reshape(b, s, -1), dtt_raw,
               conv_w.astype(F32), conv_b.reshape(1, -1).astype(F32),
               pad_h(dt_bias), pad_h(a_log),
               jnp.repeat(d_skip.astype(F32), SSD_HEAD_DIM).reshape(1, -1),
               ssd_norm_w.reshape(1, -1).astype(F32), _expand_matrix(), _tril3_matrix(), _shift_matrix())

    y = _out_ffn(x2, att.reshape(m, -1), ssd.reshape(m, -1),
                 w_out[:ATT_WIDTH].astype(BF16), w_out[ATT_WIDTH:].astype(BF16),
                 norm_mlp_w.reshape(1, d), w_ff1.astype(BF16), w_ff2.astype(BF16),
                 out_norm_w.reshape(1, d))
    return y.reshape(b, s, d)


def kernel(x, norm_mix_w, w_in, rel_bias, conv_w, conv_b, dt_bias, a_log, d_skip,
           ssd_norm_w, w_out, norm_mlp_w, w_ff1, w_ff2, norm_final_w):
    depth = w_in.shape[0]
    assert depth == 1, "final norm is fused into the single layer's last kernel"
    return _layer(x, norm_mix_w[0], w_in[0], rel_bias[0], conv_w[0], conv_b[0],
                  dt_bias[0], a_log[0], d_skip[0], ssd_norm_w[0], w_out[0],
                  norm_mlp_w[0], w_ff1[0], w_ff2[0], norm_final_w)
```

```python
import functools

import jax
import jax.numpy as jnp
from jax import lax
from jax.experimental import pallas as pl
from jax.experimental.pallas import tpu as pltpu

F32 = jnp.float32
BF16 = jnp.bfloat16

D_MODEL = 1024
CHUNK = 64
ATT_HEADS = 8
ATT_HEAD_DIM = 64
ATT_WIDTH = ATT_HEADS * ATT_HEAD_DIM
N_PREV_CHUNKS = 8
REL_CLIP = 256
SSD_HEAD_DIM = 64
SSD_WIDTH = D_MODEL
SSD_HEADS = SSD_WIDTH // SSD_HEAD_DIM
SSD_GROUPS = 2
SSD_STATE = 128
SSD_CONV = 4
CONV_DIM = SSD_WIDTH + 2 * SSD_GROUPS * SSD_STATE
MAIN_PROJ = 3 * ATT_WIDTH + SSD_WIDTH + CONV_DIM
D_FF = 4 * D_MODEL
EPS = 1e-5
NEG = -1e30

LANES = 128
SUBLANES = 8

PROJ_TM = 1024
ATT_TQ = 256
ATT_BAND = ATT_TQ + N_PREV_CHUNKS * CHUNK
SSD_L = 128
SSD_T = 512
FFN_TM = 512
VMEM_LIMIT = 56 * 1024 * 1024


def _rms(x, w):
    ms = jnp.mean(x * x, axis=-1, keepdims=True)
    return x * lax.rsqrt(ms + EPS) * w


def _in_proj_kernel(x_ref, nw_ref, wm_ref, wkt_ref,
                    q_ref, kt_ref, v_ref, z_ref, xbc_ref, dtt_ref):
    hn = _rms(x_ref[...], nw_ref[...]).astype(BF16)

    def seg(lo, hi):
        return jnp.dot(hn, wm_ref[:, lo:hi], preferred_element_type=F32).astype(BF16)

    a = ATT_WIDTH
    q_ref[...] = seg(0, a)
    ktd = lax.dot_general(wkt_ref[...], hn, (((1,), (1,)), ((), ())),
                          preferred_element_type=F32)
    kt_ref[...] = ktd[:a, :].astype(BF16)
    dtt_ref[...] = ktd[a:, :]
    v_ref[...] = seg(a, 2 * a)
    z_ref[...] = seg(2 * a, 2 * a + SSD_WIDTH)
    xbc_ref[...] = seg(2 * a + SSD_WIDTH, MAIN_PROJ - a)


def _in_proj(x2, norm_w, w_main, w_ktd):
    m = x2.shape[0]
    tm = PROJ_TM
    row = lambda i: (i, 0)
    col = lambda i: (0, i)
    const = lambda i: (0, 0)
    once = pl.Buffered(1)
    out_shape = (
        jax.ShapeDtypeStruct((m, ATT_WIDTH), BF16),
        jax.ShapeDtypeStruct((ATT_WIDTH, m), BF16),
        jax.ShapeDtypeStruct((m, ATT_WIDTH), BF16),
        jax.ShapeDtypeStruct((m, SSD_WIDTH), BF16),
        jax.ShapeDtypeStruct((m, CONV_DIM), BF16),
        jax.ShapeDtypeStruct((SSD_HEADS, m), F32),
    )
    return pl.pallas_call(
        _in_proj_kernel,
        out_shape=out_shape,
        grid=(m // tm,),
        in_specs=[
            pl.BlockSpec((tm, D_MODEL), row),
            pl.BlockSpec((1, D_MODEL), const),
            pl.BlockSpec((D_MODEL, MAIN_PROJ - ATT_WIDTH), const, pipeline_mode=once),
            pl.BlockSpec((ATT_WIDTH + SSD_HEADS, D_MODEL), const, pipeline_mode=once),
        ],
        out_specs=(
            pl.BlockSpec((tm, ATT_WIDTH), row),
            pl.BlockSpec((ATT_WIDTH, tm), col),
            pl.BlockSpec((tm, ATT_WIDTH), row),
            pl.BlockSpec((tm, SSD_WIDTH), row),
            pl.BlockSpec((tm, CONV_DIM), row),
            pl.BlockSpec((SSD_HEADS, tm), col),
        ),
        compiler_params=pltpu.CompilerParams(
            dimension_semantics=("parallel",), vmem_limit_bytes=VMEM_LIMIT),
        name="in_proj",
    )(x2, norm_w, w_main, w_ktd)


ATT_SLOTS = ATT_BAND // ATT_TQ
ATT_NB = 2
ATT_RING = ATT_NB + ATT_SLOTS - 1
ATT_AHEAD = 1
PAIR = 2 * ATT_HEAD_DIM
LOG2E = 1.4426950408889634


def _bias_table_kernel(r_ref, o_ref):
    x = jnp.broadcast_to(r_ref[0], (ATT_TQ, ATT_BAND + ATT_TQ))
    rolled = pltpu.roll(x, 0, 1, stride=1, stride_axis=0)
    t = rolled[:, ATT_TQ:]
    qc = lax.broadcasted_iota(jnp.int32, (ATT_TQ, ATT_BAND), 0) // CHUNK
    kc = lax.broadcasted_iota(jnp.int32, (ATT_TQ, ATT_BAND), 1) // CHUNK
    valid = (kc >= qc) & (kc <= qc + N_PREV_CHUNKS)
    valid = valid & (pl.program_id(0) < ATT_HEADS)
    o_ref[0] = jnp.where(valid, t * LOG2E, NEG)


def _bias_table(rel_bias):
    rb = rel_bias.astype(F32).T
    n = ATT_BAND + ATT_TQ
    far = jnp.broadcast_to(rb[:, 2 * REL_CLIP:], (ATT_HEADS, n - 2 * REL_CLIP + 1))
    near = rb[:, 2 * REL_CLIP - 1:0:-1]
    r = jnp.concatenate([far, near], axis=1)
    r = jnp.concatenate([r, jnp.zeros((1, n), F32)], axis=0).reshape(ATT_HEADS + 1, 1, n)
    return pl.pallas_call(
        _bias_table_kernel,
        out_shape=jax.ShapeDtypeStruct((ATT_HEADS + 1, ATT_TQ, ATT_BAND), F32),
        grid=(ATT_HEADS + 1,),
        in_specs=[pl.BlockSpec((1, 1, n), lambda h: (h, 0, 0))],
        out_specs=pl.BlockSpec((1, ATT_TQ, ATT_BAND), lambda h: (h, 0, 0)),
        compiler_params=pltpu.CompilerParams(dimension_semantics=("parallel",)),
        name="bias_table",
    )(r)


def _attn_kernel(q_ref, kt_ref, v_ref, tab_ref, o_ref, kring, vring):
    i = pl.program_id(1)
    tq = ATT_TQ
    hd = ATT_HEAD_DIM
    lane = lax.broadcasted_iota(jnp.int32, (tq, PAIR), 1)
    low = lane < hd

    @pl.when(i == 0)
    def _():
        kring[...] = jnp.zeros(kring.shape, BF16)
        vring[...] = jnp.zeros(vring.shape, BF16)

    for j in range(ATT_NB):
        slot_new = lax.rem(ATT_NB * i + j, ATT_RING)
        cols = slice(j * tq, (j + 1) * tq)
        for h in range(ATT_HEADS):
            r0 = h * PAIR + (h % 2) * hd
            kring[slot_new, r0:r0 + hd, :] = kt_ref[h * hd:(h + 1) * hd, cols]
        for p in range(ATT_HEADS // 2):
            vp = v_ref[0, cols, p * PAIR:(p + 1) * PAIR]
            one = jnp.ones_like(vp)
            vring[slot_new, :, (2 * p) * PAIR:(2 * p + 1) * PAIR] = jnp.where(low, vp, one)
            vring[slot_new, :, (2 * p + 1) * PAIR:(2 * p + 2) * PAIR] = jnp.where(low, one, vp)

    def band(j):
        n = [ATT_NB * i + j - (ATT_SLOTS - 1) + m for m in range(ATT_SLOTS)]
        slots = [lax.rem(nm + ATT_RING, ATT_RING) for nm in n]
        masked = [nm < 0 for nm in n]
        return slots, masked

    bands = [band(j) for j in range(ATT_NB)]

    def scores(j, h):
        slots, masked = bands[j]
        p = h // 2
        qp = q_ref[0, j * tq:(j + 1) * tq, p * PAIR:(p + 1) * PAIR]
        return [jnp.dot(qp, kring[slots[m], h * PAIR:(h + 1) * PAIR, :],
                        preferred_element_type=F32)
                + tab_ref[jnp.where(masked[m], ATT_HEADS, h), :, m * tq:(m + 1) * tq]
                for m in range(ATT_SLOTS)]

    items = [(j, h) for j in range(ATT_NB) for h in range(ATT_HEADS)]
    queue = [scores(*it) for it in items[:ATT_AHEAD]]
    res = []
    for idx, (j, h) in enumerate(items):
        s = queue.pop(0)
        if idx + ATT_AHEAD < len(items):
            queue.append(scores(*items[idx + ATT_AHEAD]))
        slots, _ = bands[j]
        mx = functools.reduce(jnp.maximum, s)
        mx = jnp.max(mx, axis=-1, keepdims=True)
        acc = None
        for m in range(ATT_SLOTS):
            e = jnp.exp2(s[m] - mx).astype(BF16)
            pv = jnp.dot(e, vring[slots[m], :, h * PAIR:(h + 1) * PAIR],
                         preferred_element_type=F32)
            acc = pv if acc is None else acc + pv
        res.append(acc)
        if h % 2 == 1:
            p = h // 2
            num = jnp.where(low, res[0], res[1])
            den = jnp.where(low, pltpu.roll(res[0], hd, 1), pltpu.roll(res[1], hd, 1))
            o_ref[0, j * tq:(j + 1) * tq, p * PAIR:(p + 1) * PAIR] = (num / den).astype(BF16)
            res = []


def _attention(q, kt, v, table):
    b, s, _ = q.shape
    tq = ATT_TQ
    ts = ATT_NB * tq
    nstep = s // ts
    blk = lambda bi, i: (bi, i, 0)
    return pl.pallas_call(
        _attn_kernel,
        out_shape=jax.ShapeDtypeStruct((b, s, ATT_WIDTH), BF16),
        grid=(b, nstep),
        in_specs=[
            pl.BlockSpec((1, ts, ATT_WIDTH), blk),
            pl.BlockSpec((ATT_WIDTH, ts), lambda bi, i: (0, bi * nstep + i)),
            pl.BlockSpec((1, ts, ATT_WIDTH), blk),
            pl.BlockSpec((ATT_HEADS + 1, tq, ATT_BAND), lambda bi, i: (0, 0, 0),
                         pipeline_mode=pl.Buffered(1)),
        ],
        out_specs=pl.BlockSpec((1, ts, ATT_WIDTH), blk),
        scratch_shapes=[
            pltpu.VMEM((ATT_RING, ATT_HEADS * PAIR, tq), BF16),
            pltpu.VMEM((ATT_RING, tq, ATT_HEADS * PAIR), BF16),
        ],
        compiler_params=pltpu.CompilerParams(
            dimension_semantics=("parallel", "arbitrary"),
            vmem_limit_bytes=VMEM_LIMIT),
        name="band_attn",
    )(q, kt, v, table)


def _split3(v):
    v1 = v.astype(BF16).astype(F32)
    r1 = v - v1
    v2 = r1.astype(BF16).astype(F32)
    v3 = (r1 - v2).astype(BF16).astype(F32)
    return v1, v2, v3


def _expand_heads(v, e3_ref):
    lane = lax.broadcasted_iota(jnp.int32, v.shape, 1)
    v = jnp.where(lane < SSD_HEADS, v, 0.0)
    v1, v2, v3 = _split3(v)
    packed = v1 + pltpu.roll(v2, SSD_HEADS, 1) + pltpu.roll(v3, 2 * SSD_HEADS, 1)
    return jnp.dot(packed.astype(BF16), e3_ref[...], preferred_element_type=F32)


def _silu(x):
    h = 0.5 * x
    return h + h * jnp.tanh(h)


CONV_TILE = 256
N_CONV_TILES = CONV_DIM // CONV_TILE


def _conv_tile(c, k, cw_ref, cb_ref, shift_ref, ubuf):
    L = SSD_L
    ch = slice(k * CONV_TILE, (k + 1) * CONV_TILE)
    win = ubuf[c * L:c * L + 2 * L, ch]
    sh = jnp.dot(shift_ref[...], win, preferred_element_type=F32)
    acc = cb_ref[:, ch] + cw_ref[SSD_CONV - 1:SSD_CONV, ch] * win[L:, :].astype(F32)
    for d in range(1, SSD_CONV):
        acc = acc + cw_ref[SSD_CONV - 1 - d:SSD_CONV - d, ch] * sh[(d - 1) * L:d * L, :]
    return _silu(acc)


def _decay_cumsum(c, dtt_ref, dtb_ref, alog_ref, tril3_ref):
    L = SSD_L
    rows = slice(c * L, (c + 1) * L)
    raw_t = jnp.concatenate([dtt_ref[:, rows], jnp.zeros((LANES - SSD_HEADS, L), F32)], axis=0)
    u = raw_t.T + dtb_ref[...]
    dt = jnp.maximum(u, 0.0) + jnp.log1p(jnp.exp(-jnp.abs(u)))
    a = dt * (-jnp.exp(alog_ref[...]) * LOG2E)
    a1, a2, a3 = _split3(a)
    a_parts = jnp.concatenate([a1, a2, a3], axis=0).astype(BF16)
    a_cum = jnp.dot(tril3_ref[...], a_parts, preferred_element_type=F32)
    return dt, a_cum


def _decay_weights(dt, a_cum, e3_ref):
    L = SSD_L
    a_last = a_cum[L - 1:L, :]
    w = dt * jnp.exp2(a_last - a_cum)
    w_exp = _expand_heads(w, e3_ref)
    cd = _expand_heads(jnp.broadcast_to(jnp.exp2(a_last), (SUBLANES, LANES)), e3_ref)[0:1, :]
    return w_exp, cd


def _ssd_front(tiles, dt, a_cum, w_exp, cd):
    xc = jnp.concatenate(tiles, axis=1)
    xs = xc[:, :SSD_WIDTH]
    xs_bf = xs.astype(BF16)
    bm = xc[:, SSD_WIDTH:SSD_WIDTH + SSD_GROUPS * SSD_STATE]
    cm = xc[:, SSD_WIDTH + SSD_GROUPS * SSD_STATE:]
    xd = (xs * w_exp).astype(BF16)
    return xs, xs_bf, bm, cm, a_cum, a_cum.T, dt.T, jnp.exp2(a_cum), xd, cd


def _ssd_back(c, front, z_ref, dexp_ref, nw_ref, y_ref, state, ybuf, causal, low, hooks):
    L = SSD_L
    gw = SSD_WIDTH // SSD_GROUPS
    ppg = SSD_HEADS // SSD_GROUPS // 2
    rows = slice(c * L, (c + 1) * L)
    xs, xs_bf, bm, cm, a_cum, a_cum_t, dt_t, exp_acum, xd, cd = front

    zero = jnp.zeros((L, PAIR), BF16)
    for g in range(SSD_GROUPS):
        bg = bm[:, g * SSD_STATE:(g + 1) * SSD_STATE]
        cg = cm[:, g * SSD_STATE:(g + 1) * SSD_STATE]
        cb = lax.dot_general(cg.astype(BF16), bg.astype(BF16), (((1,), (1,)), ((), ())),
                             preferred_element_type=F32)
        sg = state[g]
        sg_bf = sg.astype(BF16)
        for q in range(ppg):
            p = g * ppg + q
            lhs = []
            for h in (2 * p, 2 * p + 1):
                seg = a_cum[:, h:h + 1] - a_cum_t[h:h + 1, :]
                dec = jnp.where(causal, jnp.exp2(seg), 0.0)
                lhs.append((cb * dec * dt_t[h:h + 1, :]).astype(BF16))
            for h in (2 * p, 2 * p + 1):
                lhs.append((cg * exp_acum[:, h:h + 1]).astype(BF16))
            xp = xs_bf[:, p * PAIR:(p + 1) * PAIR]
            sp = sg_bf[:, q * PAIR:(q + 1) * PAIR]
            rhs = jnp.concatenate([jnp.where(low, xp, zero), jnp.where(low, zero, xp),
                                   jnp.where(low, sp, zero), jnp.where(low, zero, sp)], axis=0)
            yp = jnp.dot(jnp.concatenate(lhs, axis=1), rhs, preferred_element_type=F32)
            yp = yp + dexp_ref[:, p * PAIR:(p + 1) * PAIR] * xs[:, p * PAIR:(p + 1) * PAIR]
            zf = z_ref[0, rows, p * PAIR:(p + 1) * PAIR].astype(F32)
            ybuf[:, p * PAIR:(p + 1) * PAIR] = yp * _silu(zf)
            if p < len(hooks):
                hooks[p]()
        s_new = jnp.dot(bg.T.astype(BF16), xd[:, g * gw:(g + 1) * gw],
                        preferred_element_type=F32)
        state[g] = sg * cd[:, g * gw:(g + 1) * gw] + s_new

    for g in range(SSD_GROUPS):
        yg = ybuf[:, g * gw:(g + 1) * gw]
        ms = jnp.mean(yg * yg, axis=-1, keepdims=True)
        y_ref[0, rows, g * gw:(g + 1) * gw] = (
            yg * lax.rsqrt(ms + EPS) * nw_ref[:, g * gw:(g + 1) * gw]).astype(BF16)


def _ssd_kernel(z_ref, xbc_ref, dtt_ref, cw_ref, cb_ref, dtb_ref, alog_ref,
                dexp_ref, nw_ref, e3_ref, tril3_ref, shift_ref, y_ref, ubuf, state, ybuf):
    j = pl.program_id(1)
    T = SSD_T
    L = SSD_L

    @pl.when(j == 0)
    def _():
        ubuf[0:L, :] = jnp.zeros((L, CONV_DIM), BF16)
        state[...] = jnp.zeros(state.shape, F32)

    @pl.when(j > 0)
    def _():
        ubuf[0:L, :] = ubuf[T:T + L, :]

    ubuf[L:T + L, :] = xbc_ref[0]

    row = lax.broadcasted_iota(jnp.int32, (L, L), 0)
    col = lax.broadcasted_iota(jnp.int32, (L, L), 1)
    causal = row >= col
    low = lax.broadcasted_iota(jnp.int32, (L, PAIR), 1) < SSD_HEAD_DIM
    def conv(c, k):
        return _conv_tile(c, k, cw_ref, cb_ref, shift_ref, ubuf)

    def cumsum(c):
        return _decay_cumsum(c, dtt_ref, dtb_ref, alog_ref, tril3_ref)

    n = T // L
    dt, a_cum = cumsum(0)
    tiles = [conv(0, k) for k in range(N_CONV_TILES)]
    cur = _ssd_front(tiles, dt, a_cum, *_decay_weights(dt, a_cum, e3_ref))
    for c in range(n):
        nxt = {}
        hooks = []
        if c + 1 < n:
            def h_cumsum(cc=c + 1):
                nxt["dt"], nxt["a_cum"] = cumsum(cc)

            def h_conv(k, cc=c + 1):
                nxt.setdefault("tiles", []).append(conv(cc, k))

            def h_weights():
                nxt["w"] = _decay_weights(nxt["dt"], nxt["a_cum"], e3_ref)

            hooks = ([h_cumsum] + [functools.partial(h_conv, k) for k in range(N_CONV_TILES)]
                     + [h_weights])
        _ssd_back(c, cur, z_ref, dexp_ref, nw_ref, y_ref, state, ybuf, causal, low, hooks)
        if c + 1 < n:
            cur = _ssd_front(nxt["tiles"], nxt["dt"], nxt["a_cum"], *nxt["w"])


def _ssd(z, xbc, dtt_raw, conv_w, conv_b, dt_bias_p, a_log_p, d_exp, norm_w, e3, tril3, shift):
    b, s, _ = z.shape
    T = SSD_T
    nblk = s // T
    blk = lambda bi, j: (bi, j, 0)
    const = lambda bi, j: (0, 0)
    once = pl.Buffered(1)
    return pl.pallas_call(
        _ssd_kernel,
        out_shape=jax.ShapeDtypeStruct((b, s, SSD_WIDTH), BF16),
        grid=(b, s // T),
        in_specs=[
            pl.BlockSpec((1, T, SSD_WIDTH), blk),
            pl.BlockSpec((1, T, CONV_DIM), blk),
            pl.BlockSpec((SSD_HEADS, T), lambda bi, j: (0, bi * nblk + j)),
            pl.BlockSpec((SSD_CONV, CONV_DIM), const),
            pl.BlockSpec((1, CONV_DIM), const),
            pl.BlockSpec((1, LANES), const),
            pl.BlockSpec((1, LANES), const),
            pl.BlockSpec((1, SSD_WIDTH), const),
            pl.BlockSpec((1, SSD_WIDTH), const),
            pl.BlockSpec((LANES, SSD_WIDTH), const, pipeline_mode=once),
            pl.BlockSpec((SSD_L, 3 * SSD_L), const, pipeline_mode=once),
            pl.BlockSpec(((SSD_CONV - 1) * SSD_L, 2 * SSD_L), const, pipeline_mode=once),
        ],
        out_specs=pl.BlockSpec((1, T, SSD_WIDTH), blk),
        scratch_shapes=[
            pltpu.VMEM((T + SSD_L, CONV_DIM), BF16),
            pltpu.VMEM((SSD_GROUPS, SSD_STATE, SSD_WIDTH // SSD_GROUPS), F32),
            pltpu.VMEM((SSD_L, SSD_WIDTH), F32),
        ],
        compiler_params=pltpu.CompilerParams(
            dimension_semantics=("parallel", "arbitrary"),
            vmem_limit_bytes=VMEM_LIMIT),
        name="ssd_mixer",
    )(z, xbc, dtt_raw, conv_w, conv_b, dt_bias_p, a_log_p, d_exp, norm_w, e3, tril3, shift)


def _shift_matrix():
    r = jnp.arange((SSD_CONV - 1) * SSD_L)[:, None]
    c = jnp.arange(2 * SSD_L)[None, :]
    return (c == SSD_L + r % SSD_L - (r // SSD_L + 1)).astype(BF16)


def _tril3_matrix():
    r = jnp.arange(SSD_L)[:, None]
    c = jnp.arange(3 * SSD_L)[None, :] % SSD_L
    return (r >= c).astype(BF16)


def _expand_matrix():
    r = jnp.arange(LANES)[:, None]
    c = jnp.arange(SSD_WIDTH)[None, :]
    hit = (r < 3 * SSD_HEADS) & ((r % SSD_HEADS) == c // SSD_HEAD_DIM)
    return hit.astype(BF16)


def _out_ffn_kernel(x_ref, att_ref, ssd_ref, woa_ref, wos_ref, nw_ref,
                    w1_ref, w2_ref, fw_ref, o_ref):
    h = (x_ref[...]
         + jnp.dot(att_ref[...], woa_ref[...], preferred_element_type=F32)
         + jnp.dot(ssd_ref[...], wos_ref[...], preferred_element_type=F32))
    hn = _rms(h, nw_ref[...]).astype(BF16)
    hid = jnp.dot(hn, w1_ref[...], preferred_element_type=F32)
    hid = jnp.square(jnp.maximum(hid, 0.0)).astype(BF16)
    h = h + jnp.dot(hid, w2_ref[...], preferred_element_type=F32)
    o_ref[...] = _rms(h, fw_ref[...])


def _out_ffn(x2, att2, ssd2, wo_a, wo_s, norm_w, w1, w2, final_w):
    m = x2.shape[0]
    tm = FFN_TM
    row = lambda i: (i, 0)
    const = lambda i: (0, 0)
    once = pl.Buffered(1)
    return pl.pallas_call(
        _out_ffn_kernel,
        out_shape=jax.ShapeDtypeStruct((m, D_MODEL), F32),
        grid=(m // tm,),
        in_specs=[
            pl.BlockSpec((tm, D_MODEL), row),
            pl.BlockSpec((tm, ATT_WIDTH), row),
            pl.BlockSpec((tm, SSD_WIDTH), row),
            pl.BlockSpec((ATT_WIDTH, D_MODEL), const, pipeline_mode=once),
            pl.BlockSpec((SSD_WIDTH, D_MODEL), const, pipeline_mode=once),
            pl.BlockSpec((1, D_MODEL), const),
            pl.BlockSpec((D_MODEL, D_FF), const, pipeline_mode=once),
            pl.BlockSpec((D_FF, D_MODEL), const, pipeline_mode=once),
            pl.BlockSpec((1, D_MODEL), const),
        ],
        out_specs=pl.BlockSpec((tm, D_MODEL), row),
        compiler_params=pltpu.CompilerParams(
            dimension_semantics=("parallel",), vmem_limit_bytes=VMEM_LIMIT),
        name="out_ffn",
    )(x2, att2, ssd2, wo_a, wo_s, norm_w, w1, w2, final_w)


def _layer(x, norm_mix_w, w_in, rel_bias, conv_w, conv_b, dt_bias, a_log, d_skip,
           ssd_norm_w, w_out, norm_mlp_w, w_ff1, w_ff2, out_norm_w):
    b, s, d = x.shape
    m = b * s
    x2 = x.reshape(m, d)

    scale = ATT_HEAD_DIM ** -0.5 * LOG2E
    w_main = jnp.concatenate(
        [w_in[:, :ATT_WIDTH] * scale, w_in[:, 2 * ATT_WIDTH:MAIN_PROJ]], axis=1).astype(BF16)
    w_ktd = jnp.concatenate([w_in[:, ATT_WIDTH:2 * ATT_WIDTH], w_in[:, MAIN_PROJ:]],
                            axis=1).T.astype(BF16)
    pad_h = lambda v: jnp.pad(v.astype(F32), (0, LANES - SSD_HEADS)).reshape(1, LANES)

    q, kt, v, z, xbc, dtt_raw = _in_proj(x2, norm_mix_w.reshape(1, d), w_main, w_ktd)

    att = _attention(q.reshape(b, s, -1), kt, v.reshape(b, s, -1), _bias_table(rel_bias))
    ssd = _ssd(z.reshape(b, s, -1), xbc.reshape(b, s, -1), dtt_raw,
               conv_w.astype(F32), conv_b.reshape(1, -1).astype(F32),
               pad_h(dt_bias), pad_h(a_log),
               jnp.repeat(d_skip.astype(F32), SSD_HEAD_DIM).reshape(1, -1),
               ssd_norm_w.reshape(1, -1).astype(F32), _expand_matrix(), _tril3_matrix(), _shift_matrix())

    y = _out_ffn(x2, att.reshape(m, -1), ssd.reshape(m, -1),
                 w_out[:ATT_WIDTH].astype(BF16), w_out[ATT_WIDTH:].astype(BF16),
                 norm_mlp_w.reshape(1, d), w_ff1.astype(BF16), w_ff2.astype(BF16),
                 out_norm_w.reshape(1, d))
    return y.reshape(b, s, d)


def kernel(x, norm_mix_w, w_in, rel_bias, conv_w, conv_b, dt_bias, a_log, d_skip,
           ssd_norm_w, w_out, norm_mlp_w, w_ff1, w_ff2, norm_final_w):
    depth = w_in.shape[0]
    assert depth == 1, "final norm is fused into the single layer's last kernel"
    return _layer(x, norm_mix_w[0], w_in[0], rel_bias[0], conv_w[0], conv_b[0],
                  dt_bias[0], a_log[0], d_skip[0], ssd_norm_w[0], w_out[0],
                  norm_mlp_w[0], w_ff1[0], w_ff2[0], norm_final_w)
```

```python
import functools

import jax
import jax.numpy as jnp
from jax import lax
from jax.experimental import pallas as pl
from jax.experimental.pallas import tpu as pltpu

F32 = jnp.float32
BF16 = jnp.bfloat16

D_MODEL = 1024
CHUNK = 64
ATT_HEADS = 8
ATT_HEAD_DIM = 64
ATT_WIDTH = ATT_HEADS * ATT_HEAD_DIM
N_PREV_CHUNKS = 8
REL_CLIP = 256
SSD_HEAD_DIM = 64
SSD_WIDTH = D_MODEL
SSD_HEADS = SSD_WIDTH // SSD_HEAD_DIM
SSD_GROUPS = 2
SSD_STATE = 128
SSD_CONV = 4
CONV_DIM = SSD_WIDTH + 2 * SSD_GROUPS * SSD_STATE
MAIN_PROJ = 3 * ATT_WIDTH + SSD_WIDTH + CONV_DIM
D_FF = 4 * D_MODEL
EPS = 1e-5
NEG = -1e30

LANES = 128
SUBLANES = 8

PROJ_TM = 1024
ATT_TQ = 256
ATT_BAND = ATT_TQ + N_PREV_CHUNKS * CHUNK
SSD_L = 128
SSD_T = 512
MIX_PAIRS_PER_ITEM = 2
FFN_TM = 512
VMEM_LIMIT = 56 * 1024 * 1024


def _rms(x, w):
    ms = jnp.mean(x * x, axis=-1, keepdims=True)
    return x * lax.rsqrt(ms + EPS) * w


def _in_proj_kernel(x_ref, nw_ref, wm_ref, wkt_ref,
                    q_ref, kt_ref, v_ref, z_ref, xbc_ref, dtt_ref):
    hn = _rms(x_ref[...], nw_ref[...]).astype(BF16)

    def seg(lo, hi):
        return jnp.dot(hn, wm_ref[:, lo:hi], preferred_element_type=F32).astype(BF16)

    a = ATT_WIDTH
    q_ref[...] = seg(0, a)
    ktd = lax.dot_general(wkt_ref[...], hn, (((1,), (1,)), ((), ())),
                          preferred_element_type=F32)
    kt_ref[...] = ktd[:a, :].astype(BF16)
    dtt_ref[...] = ktd[a:, :]
    v_ref[...] = seg(a, 2 * a)
    z_ref[...] = seg(2 * a, 2 * a + SSD_WIDTH)
    xbc_ref[...] = seg(2 * a + SSD_WIDTH, MAIN_PROJ - a)


def _in_proj(x2, norm_w, w_main, w_ktd):
    m = x2.shape[0]
    tm = PROJ_TM
    row = lambda i: (i, 0)
    col = lambda i: (0, i)
    const = lambda i: (0, 0)
    once = pl.Buffered(1)
    out_shape = (
        jax.ShapeDtypeStruct((m, ATT_WIDTH), BF16),
        jax.ShapeDtypeStruct((ATT_WIDTH, m), BF16),
        jax.ShapeDtypeStruct((m, ATT_WIDTH), BF16),
        jax.ShapeDtypeStruct((m, SSD_WIDTH), BF16),
        jax.ShapeDtypeStruct((m, CONV_DIM), BF16),
        jax.ShapeDtypeStruct((SSD_HEADS, m), F32),
    )
    return pl.pallas_call(
        _in_proj_kernel,
        out_shape=out_shape,
        grid=(m // tm,),
        in_specs=[
            pl.BlockSpec((tm, D_MODEL), row),
            pl.BlockSpec((1, D_MODEL), const),
            pl.BlockSpec((D_MODEL, MAIN_PROJ - ATT_WIDTH), const, pipeline_mode=once),
            pl.BlockSpec((ATT_WIDTH + SSD_HEADS, D_MODEL), const, pipeline_mode=once),
        ],
        out_specs=(
            pl.BlockSpec((tm, ATT_WIDTH), row),
            pl.BlockSpec((ATT_WIDTH, tm), col),
            pl.BlockSpec((tm, ATT_WIDTH), row),
            pl.BlockSpec((tm, SSD_WIDTH), row),
            pl.BlockSpec((tm, CONV_DIM), row),
            pl.BlockSpec((SSD_HEADS, tm), col),
        ),
        compiler_params=pltpu.CompilerParams(
            dimension_semantics=("parallel",), vmem_limit_bytes=VMEM_LIMIT),
        name="in_proj",
    )(x2, norm_w, w_main, w_ktd)


ATT_SLOTS = ATT_BAND // ATT_TQ
ATT_NB = 2
ATT_RING = ATT_NB + ATT_SLOTS - 1
ATT_AHEAD = 1
PAIR = 2 * ATT_HEAD_DIM
LOG2E = 1.4426950408889634


def _bias_table_kernel(r_ref, o_ref):
    x = jnp.broadcast_to(r_ref[0], (ATT_TQ, ATT_BAND + ATT_TQ))
    rolled = pltpu.roll(x, 0, 1, stride=1, stride_axis=0)
    t = rolled[:, ATT_TQ:]
    qc = lax.broadcasted_iota(jnp.int32, (ATT_TQ, ATT_BAND), 0) // CHUNK
    kc = lax.broadcasted_iota(jnp.int32, (ATT_TQ, ATT_BAND), 1) // CHUNK
    valid = (kc >= qc) & (kc <= qc + N_PREV_CHUNKS)
    valid = valid & (pl.program_id(0) < ATT_HEADS)
    o_ref[0] = jnp.where(valid, t * LOG2E, NEG)


def _bias_table(rel_bias):
    rb = rel_bias.astype(F32).T
    n = ATT_BAND + ATT_TQ
    far = jnp.broadcast_to(rb[:, 2 * REL_CLIP:], (ATT_HEADS, n - 2 * REL_CLIP + 1))
    near = rb[:, 2 * REL_CLIP - 1:0:-1]
    r = jnp.concatenate([far, near], axis=1)
    r = jnp.concatenate([r, jnp.zeros((1, n), F32)], axis=0).reshape(ATT_HEADS + 1, 1, n)
    return pl.pallas_call(
        _bias_table_kernel,
        out_shape=jax.ShapeDtypeStruct((ATT_HEADS + 1, ATT_TQ, ATT_BAND), F32),
        grid=(ATT_HEADS + 1,),
        in_specs=[pl.BlockSpec((1, 1, n), lambda h: (h, 0, 0))],
        out_specs=pl.BlockSpec((1, ATT_TQ, ATT_BAND), lambda h: (h, 0, 0)),
        compiler_params=pltpu.CompilerParams(dimension_semantics=("parallel",)),
        name="bias_table",
    )(r)


def _attn_steps(i, q_ref, kt_ref, v_ref, tab_ref, o_ref, kring, vring):
    tq = ATT_TQ
    hd = ATT_HEAD_DIM
    lane = lax.broadcasted_iota(jnp.int32, (tq, PAIR), 1)
    low = lane < hd

    @pl.when(i == 0)
    def _():
        kring[...] = jnp.zeros(kring.shape, BF16)
        vring[...] = jnp.zeros(vring.shape, BF16)

    for j in range(ATT_NB):
        slot_new = lax.rem(ATT_NB * i + j, ATT_RING)
        cols = slice(j * tq, (j + 1) * tq)
        for h in range(ATT_HEADS):
            r0 = h * PAIR + (h % 2) * hd
            kring[slot_new, r0:r0 + hd, :] = kt_ref[h * hd:(h + 1) * hd, cols]
        for p in range(ATT_HEADS // 2):
            vp = v_ref[0, cols, p * PAIR:(p + 1) * PAIR]
            one = jnp.ones_like(vp)
            vring[slot_new, :, (2 * p) * PAIR:(2 * p + 1) * PAIR] = jnp.where(low, vp, one)
            vring[slot_new, :, (2 * p + 1) * PAIR:(2 * p + 2) * PAIR] = jnp.where(low, one, vp)

    def band(j):
        n = [ATT_NB * i + j - (ATT_SLOTS - 1) + m for m in range(ATT_SLOTS)]
        slots = [lax.rem(nm + ATT_RING, ATT_RING) for nm in n]
        masked = [nm < 0 for nm in n]
        return slots, masked

    bands = [band(j) for j in range(ATT_NB)]

    def scores(j, h):
        slots, masked = bands[j]
        p = h // 2
        qp = q_ref[0, j * tq:(j + 1) * tq, p * PAIR:(p + 1) * PAIR]
        return [jnp.dot(qp, kring[slots[m], h * PAIR:(h + 1) * PAIR, :],
                        preferred_element_type=F32)
                + tab_ref[jnp.where(masked[m], ATT_HEADS, h), :, m * tq:(m + 1) * tq]
                for m in range(ATT_SLOTS)]

    items = [(j, h) for j in range(ATT_NB) for h in range(ATT_HEADS)]
    queue = [scores(*it) for it in items[:ATT_AHEAD]]
    res = []
    yield
    for idx, (j, h) in enumerate(items):
        s = queue.pop(0)
        if idx + ATT_AHEAD < len(items):
            queue.append(scores(*items[idx + ATT_AHEAD]))
        slots, _ = bands[j]
        mx = functools.reduce(jnp.maximum, s)
        mx = jnp.max(mx, axis=-1, keepdims=True)
        acc = None
        for m in range(ATT_SLOTS):
            e = jnp.exp2(s[m] - mx).astype(BF16)
            pv = jnp.dot(e, vring[slots[m], :, h * PAIR:(h + 1) * PAIR],
                         preferred_element_type=F32)
            acc = pv if acc is None else acc + pv
        res.append(acc)
        if h % 2 == 1:
            p = h // 2
            num = jnp.where(low, res[0], res[1])
            den = jnp.where(low, pltpu.roll(res[0], hd, 1), pltpu.roll(res[1], hd, 1))
            o_ref[0, j * tq:(j + 1) * tq, p * PAIR:(p + 1) * PAIR] = (num / den).astype(BF16)
            res = []
        yield


def _split3(v):
    v1 = v.astype(BF16).astype(F32)
    r1 = v - v1
    v2 = r1.astype(BF16).astype(F32)
    v3 = (r1 - v2).astype(BF16).astype(F32)
    return v1, v2, v3


def _expand_heads(v, e3_ref):
    lane = lax.broadcasted_iota(jnp.int32, v.shape, 1)
    v = jnp.where(lane < SSD_HEADS, v, 0.0)
    v1, v2, v3 = _split3(v)
    packed = v1 + pltpu.roll(v2, SSD_HEADS, 1) + pltpu.roll(v3, 2 * SSD_HEADS, 1)
    return jnp.dot(packed.astype(BF16), e3_ref[...], preferred_element_type=F32)


def _silu(x):
    h = 0.5 * x
    return h + h * jnp.tanh(h)


CONV_TILE = 256
N_CONV_TILES = CONV_DIM // CONV_TILE


def _conv_tile(c, k, cw_ref, cb_ref, shift_ref, ubuf):
    L = SSD_L
    ch = slice(k * CONV_TILE, (k + 1) * CONV_TILE)
    win = ubuf[c * L:c * L + 2 * L, ch]
    sh = jnp.dot(shift_ref[...], win, preferred_element_type=F32)
    acc = cb_ref[:, ch] + cw_ref[SSD_CONV - 1:SSD_CONV, ch] * win[L:, :].astype(F32)
    for d in range(1, SSD_CONV):
        acc = acc + cw_ref[SSD_CONV - 1 - d:SSD_CONV - d, ch] * sh[(d - 1) * L:d * L, :]
    return _silu(acc)


def _decay_cumsum(c, dtt_ref, dtb_ref, alog_ref, tril3_ref):
    L = SSD_L
    rows = slice(c * L, (c + 1) * L)
    raw_t = jnp.concatenate([dtt_ref[:, rows], jnp.zeros((LANES - SSD_HEADS, L), F32)], axis=0)
    u = raw_t.T + dtb_ref[...]
    dt = jnp.maximum(u, 0.0) + jnp.log1p(jnp.exp(-jnp.abs(u)))
    a = dt * (-jnp.exp(alog_ref[...]) * LOG2E)
    a1, a2, a3 = _split3(a)
    a_parts = jnp.concatenate([a1, a2, a3], axis=0).astype(BF16)
    a_cum = jnp.dot(tril3_ref[...], a_parts, preferred_element_type=F32)
    return dt, a_cum


def _decay_weights(dt, a_cum, e3_ref):
    L = SSD_L
    a_last = a_cum[L - 1:L, :]
    w = dt * jnp.exp2(a_last - a_cum)
    w_exp = _expand_heads(w, e3_ref)
    cd = _expand_heads(jnp.broadcast_to(jnp.exp2(a_last), (SUBLANES, LANES)), e3_ref)[0:1, :]
    return w_exp, cd


def _ssd_front(tiles, dt, a_cum, w_exp, cd):
    xc = jnp.concatenate(tiles, axis=1)
    xs = xc[:, :SSD_WIDTH]
    xs_bf = xs.astype(BF16)
    bm = xc[:, SSD_WIDTH:SSD_WIDTH + SSD_GROUPS * SSD_STATE]
    cm = xc[:, SSD_WIDTH + SSD_GROUPS * SSD_STATE:]
    xd = (xs * w_exp).astype(BF16)
    return xs, xs_bf, bm, cm, a_cum, a_cum.T, dt.T, jnp.exp2(a_cum), xd, cd


def _ssd_back(c, front, z_ref, dexp_ref, nw_ref, y_ref, state, ybuf, causal, low, hooks, tick):
    L = SSD_L
    gw = SSD_WIDTH // SSD_GROUPS
    ppg = SSD_HEADS // SSD_GROUPS // 2
    rows = slice(c * L, (c + 1) * L)
    xs, xs_bf, bm, cm, a_cum, a_cum_t, dt_t, exp_acum, xd, cd = front

    def finish(p, lhs, rhs):
        yp = jnp.dot(lhs, rhs, preferred_element_type=F32)
        yp = yp + dexp_ref[:, p * PAIR:(p + 1) * PAIR] * xs[:, p * PAIR:(p + 1) * PAIR]
        zf = z_ref[0, rows, p * PAIR:(p + 1) * PAIR].astype(F32)
        ybuf[:, p * PAIR:(p + 1) * PAIR] = yp * _silu(zf)

    zero = jnp.zeros((L, PAIR), BF16)
    pending = None
    for g in range(SSD_GROUPS):
        bg = bm[:, g * SSD_STATE:(g + 1) * SSD_STATE]
        cg = cm[:, g * SSD_STATE:(g + 1) * SSD_STATE]
        cb = lax.dot_general(cg.astype(BF16), bg.astype(BF16), (((1,), (1,)), ((), ())),
                             preferred_element_type=F32)
        sg = state[g]
        sg_bf = sg.astype(BF16)
        for q in range(ppg):
            p = g * ppg + q
            lhs = []
            for h in (2 * p, 2 * p + 1):
                seg = a_cum[:, h:h + 1] - a_cum_t[h:h + 1, :]
                dec = jnp.where(causal, jnp.exp2(seg), 0.0)
                lhs.append((cb * dec * dt_t[h:h + 1, :]).astype(BF16))
            for h in (2 * p, 2 * p + 1):
                lhs.append((cg * exp_acum[:, h:h + 1]).astype(BF16))
            xp = xs_bf[:, p * PAIR:(p + 1) * PAIR]
            sp = sg_bf[:, q * PAIR:(q + 1) * PAIR]
            rhs = jnp.concatenate([jnp.where(low, xp, zero), jnp.where(low, zero, xp),
                                   jnp.where(low, sp, zero), jnp.where(low, zero, sp)], axis=0)
            lhs = jnp.concatenate(lhs, axis=1)
            if pending is not None:
                finish(*pending)
            pending = (p, lhs, rhs)
            if p < len(hooks):
                hooks[p]()
            tick()
        finish(*pending)
        pending = None
        s_new = jnp.dot(bg.T.astype(BF16), xd[:, g * gw:(g + 1) * gw],
                        preferred_element_type=F32)
        state[g] = sg * cd[:, g * gw:(g + 1) * gw] + s_new

    for g in range(SSD_GROUPS):
        yg = ybuf[:, g * gw:(g + 1) * gw]
        ms = jnp.mean(yg * yg, axis=-1, keepdims=True)
        y_ref[0, rows, g * gw:(g + 1) * gw] = (
            yg * lax.rsqrt(ms + EPS) * nw_ref[:, g * gw:(g + 1) * gw]).astype(BF16)


def _ssd_steps(j, z_ref, xbc_ref, dtt_ref, cw_ref, cb_ref, dtb_ref, alog_ref,
               dexp_ref, nw_ref, e3_ref, tril3_ref, shift_ref, y_ref, ubuf, state, ybuf, tick):
    T = SSD_T
    L = SSD_L

    @pl.when(j == 0)
    def _():
        ubuf[0:L, :] = jnp.zeros((L, CONV_DIM), BF16)
        state[...] = jnp.zeros(state.shape, F32)

    @pl.when(j > 0)
    def _():
        ubuf[0:L, :] = ubuf[T:T + L, :]

    ubuf[L:T + L, :] = xbc_ref[0]

    row = lax.broadcasted_iota(jnp.int32, (L, L), 0)
    col = lax.broadcasted_iota(jnp.int32, (L, L), 1)
    causal = row >= col
    low = lax.broadcasted_iota(jnp.int32, (L, PAIR), 1) < SSD_HEAD_DIM
    def conv(c, k):
        return _conv_tile(c, k, cw_ref, cb_ref, shift_ref, ubuf)

    def cumsum(c):
        return _decay_cumsum(c, dtt_ref, dtb_ref, alog_ref, tril3_ref)

    n = T // L
    dt, a_cum = cumsum(0)
    tiles = [conv(0, k) for k in range(N_CONV_TILES)]
    cur = _ssd_front(tiles, dt, a_cum, *_decay_weights(dt, a_cum, e3_ref))
    for c in range(n):
        nxt = {}
        hooks = []
        if c + 1 < n:
            def h_cumsum(cc=c + 1):
                nxt["dt"], nxt["a_cum"] = cumsum(cc)

            def h_conv(k, cc=c + 1):
                nxt.setdefault("tiles", []).append(conv(cc, k))

            def h_weights():
                nxt["w"] = _decay_weights(nxt["dt"], nxt["a_cum"], e3_ref)

            hooks = ([h_cumsum] + [functools.partial(h_conv, k) for k in range(N_CONV_TILES)]
                     + [h_weights])
        _ssd_back(c, cur, z_ref, dexp_ref, nw_ref, y_ref, state, ybuf, causal, low, hooks, tick)
        if c + 1 < n:
            cur = _ssd_front(nxt["tiles"], nxt["dt"], nxt["a_cum"], *nxt["w"])


def _mix_kernel(q_ref, kt_ref, v_ref, tab_ref, z_ref, xbc_ref, dtt_ref, cw_ref, cb_ref,
                dtb_ref, alog_ref, dexp_ref, nw_ref, e3_ref, tril3_ref, shift_ref,
                att_ref, y_ref, kring, vring, ubuf, state, ybuf):
    i = pl.program_id(1)
    att = _attn_steps(i, q_ref, kt_ref, v_ref, tab_ref, att_ref, kring, vring)
    pairs_done = [0]

    def tick():
        pairs_done[0] += 1
        if pairs_done[0] % MIX_PAIRS_PER_ITEM == 0:
            next(att, None)

    next(att)
    _ssd_steps(i, z_ref, xbc_ref, dtt_ref, cw_ref, cb_ref, dtb_ref, alog_ref, dexp_ref,
               nw_ref, e3_ref, tril3_ref, shift_ref, y_ref, ubuf, state, ybuf, tick)
    for _ in att:
        pass


def _mixers(q, kt, v, table, z, xbc, dtt_raw, conv_w, conv_b, dt_bias_p, a_log_p, d_exp,
            norm_w, e3, tril3, shift):
    b, s, _ = z.shape
    T = SSD_T
    assert T == ATT_NB * ATT_TQ
    nblk = s // T
    blk = lambda bi, j: (bi, j, 0)
    colblk = lambda bi, j: (0, bi * nblk + j)
    const = lambda bi, j: (0, 0)
    once = pl.Buffered(1)
    return pl.pallas_call(
        _mix_kernel,
        out_shape=(jax.ShapeDtypeStruct((b, s, ATT_WIDTH), BF16),
                   jax.ShapeDtypeStruct((b, s, SSD_WIDTH), BF16)),
        grid=(b, nblk),
        in_specs=[
            pl.BlockSpec((1, T, ATT_WIDTH), blk),
            pl.BlockSpec((ATT_WIDTH, T), colblk),
            pl.BlockSpec((1, T, ATT_WIDTH), blk),
            pl.BlockSpec((ATT_HEADS + 1, ATT_TQ, ATT_BAND), lambda bi, j: (0, 0, 0),
                         pipeline_mode=once),
            pl.BlockSpec((1, T, SSD_WIDTH), blk),
            pl.BlockSpec((1, T, CONV_DIM), blk),
            pl.BlockSpec((SSD_HEADS, T), colblk),
            pl.BlockSpec((SSD_CONV, CONV_DIM), const),
            pl.BlockSpec((1, CONV_DIM), const),
            pl.BlockSpec((1, LANES), const),
            pl.BlockSpec((1, LANES), const),
            pl.BlockSpec((1, SSD_WIDTH), const),
            pl.BlockSpec((1, SSD_WIDTH), const),
            pl.BlockSpec((LANES, SSD_WIDTH), const, pipeline_mode=once),
            pl.BlockSpec((SSD_L, 3 * SSD_L), const, pipeline_mode=once),
            pl.BlockSpec(((SSD_CONV - 1) * SSD_L, 2 * SSD_L), const, pipeline_mode=once),
        ],
        out_specs=(pl.BlockSpec((1, T, ATT_WIDTH), blk),
                   pl.BlockSpec((1, T, SSD_WIDTH), blk)),
        scratch_shapes=[
            pltpu.VMEM((ATT_RING, ATT_HEADS * PAIR, ATT_TQ), BF16),
            pltpu.VMEM((ATT_RING, ATT_TQ, ATT_HEADS * PAIR), BF16),
            pltpu.VMEM((T + SSD_L, CONV_DIM), BF16),
            pltpu.VMEM((SSD_GROUPS, SSD_STATE, SSD_WIDTH // SSD_GROUPS), F32),
            pltpu.VMEM((SSD_L, SSD_WIDTH), F32),
        ],
        compiler_params=pltpu.CompilerParams(
            dimension_semantics=("parallel", "arbitrary"),
            vmem_limit_bytes=VMEM_LIMIT),
        name="mixers",
    )(q, kt, v, table, z, xbc, dtt_raw, conv_w, conv_b, dt_bias_p, a_log_p, d_exp, norm_w,
      e3, tril3, shift)


def _shift_matrix():
    r = jnp.arange((SSD_CONV - 1) * SSD_L)[:, None]
    c = jnp.arange(2 * SSD_L)[None, :]
    return (c == SSD_L + r % SSD_L - (r // SSD_L + 1)).astype(BF16)


def _tril3_matrix():
    r = jnp.arange(SSD_L)[:, None]
    c = jnp.arange(3 * SSD_L)[None, :] % SSD_L
    return (r >= c).astype(BF16)


def _expand_matrix():
    r = jnp.arange(LANES)[:, None]
    c = jnp.arange(SSD_WIDTH)[None, :]
    hit = (r < 3 * SSD_HEADS) & ((r % SSD_HEADS) == c // SSD_HEAD_DIM)
    return hit.astype(BF16)


def _out_ffn_kernel(x_ref, att_ref, ssd_ref, woa_ref, wos_ref, nw_ref,
                    w1_ref, w2_ref, fw_ref, o_ref):
    h = (x_ref[...]
         + jnp.dot(att_ref[...], woa_ref[...], preferred_element_type=F32)
         + jnp.dot(ssd_ref[...], wos_ref[...], preferred_element_type=F32))
    hn = _rms(h, nw_ref[...]).astype(BF16)
    hid = jnp.dot(hn, w1_ref[...], preferred_element_type=F32)
    hid = jnp.square(jnp.maximum(hid, 0.0)).astype(BF16)
    h = h + jnp.dot(hid, w2_ref[...], preferred_element_type=F32)
    o_ref[...] = _rms(h, fw_ref[...])


def _out_ffn(x2, att2, ssd2, wo_a, wo_s, norm_w, w1, w2, final_w):
    m = x2.shape[0]
    tm = FFN_TM
    row = lambda i: (i, 0)
    const = lambda i: (0, 0)
    once = pl.Buffered(1)
    return pl.pallas_call(
        _out_ffn_kernel,
        out_shape=jax.ShapeDtypeStruct((m, D_MODEL), F32),
        grid=(m // tm,),
        in_specs=[
            pl.BlockSpec((tm, D_MODEL), row),
            pl.BlockSpec((tm, ATT_WIDTH), row),
            pl.BlockSpec((tm, SSD_WIDTH), row),
            pl.BlockSpec((ATT_WIDTH, D_MODEL), const, pipeline_mode=once),
            pl.BlockSpec((SSD_WIDTH, D_MODEL), const, pipeline_mode=once),
            pl.BlockSpec((1, D_MODEL), const),
            pl.BlockSpec((D_MODEL, D_FF), const, pipeline_mode=once),
            pl.BlockSpec((D_FF, D_MODEL), const, pipeline_mode=once),
            pl.BlockSpec((1, D_MODEL), const),
        ],
        out_specs=pl.BlockSpec((tm, D_MODEL), row),
        compiler_params=pltpu.CompilerParams(
            dimension_semantics=("parallel",), vmem_limit_bytes=VMEM_LIMIT),
        name="out_ffn",
    )(x2, att2, ssd2, wo_a, wo_s, norm_w, w1, w2, final_w)


def _layer(x, norm_mix_w, w_in, rel_bias, conv_w, conv_b, dt_bias, a_log, d_skip,
           ssd_norm_w, w_out, norm_mlp_w, w_ff1, w_ff2, out_norm_w):
    b, s, d = x.shape
    m = b * s
    x2 = x.reshape(m, d)

    scale = ATT_HEAD_DIM ** -0.5 * LOG2E
    w_main = jnp.concatenate(
        [w_in[:, :ATT_WIDTH] * scale, w_in[:, 2 * ATT_WIDTH:MAIN_PROJ]], axis=1).astype(BF16)
    w_ktd = jnp.concatenate([w_in[:, ATT_WIDTH:2 * ATT_WIDTH], w_in[:, MAIN_PROJ:]],
                            axis=1).T.astype(BF16)
    pad_h = lambda v: jnp.pad(v.astype(F32), (0, LANES - SSD_HEADS)).reshape(1, LANES)

    q, kt, v, z, xbc, dtt_raw = _in_proj(x2, norm_mix_w.reshape(1, d), w_main, w_ktd)

    att, ssd = _mixers(q.reshape(b, s, -1), kt, v.reshape(b, s, -1), _bias_table(rel_bias),
                       z.reshape(b, s, -1), xbc.reshape(b, s, -1), dtt_raw,
                       conv_w.astype(F32), conv_b.reshape(1, -1).astype(F32),
                       pad_h(dt_bias), pad_h(a_log),
                       jnp.repeat(d_skip.astype(F32), SSD_HEAD_DIM).reshape(1, -1),
                       ssd_norm_w.reshape(1, -1).astype(F32),
                       _expand_matrix(), _tril3_matrix(), _shift_matrix())

    y = _out_ffn(x2, att.reshape(m, -1), ssd.reshape(m, -1),
                 w_out[:ATT_WIDTH].astype(BF16), w_out[ATT_WIDTH:].astype(BF16),
                 norm_mlp_w.reshape(1, d), w_ff1.astype(BF16), w_ff2.astype(BF16),
                 out_norm_w.reshape(1, d))
    return y.reshape(b, s, d)


def kernel(x, norm_mix_w, w_in, rel_bias, conv_w, conv_b, dt_bias, a_log, d_skip,
           ssd_norm_w, w_out, norm_mlp_w, w_ff1, w_ff2, norm_final_w):
    depth = w_in.shape[0]
    assert depth == 1, "final norm is fused into the single layer's last kernel"
    return _layer(x, norm_mix_w[0], w_in[0], rel_bias[0], conv_w[0], conv_b[0],
                  dt_bias[0], a_log[0], d_skip[0], ssd_norm_w[0], w_out[0],
                  norm_mlp_w[0], w_ff1[0], w_ff2[0], norm_final_w)
```

```python
import functools

import jax
import jax.numpy as jnp
from jax import lax
from jax.experimental import pallas as pl
from jax.experimental.pallas import tpu as pltpu

F32 = jnp.float32
BF16 = jnp.bfloat16

D_MODEL = 1024
CHUNK = 64
ATT_HEADS = 8
ATT_HEAD_DIM = 64
ATT_WIDTH = ATT_HEADS * ATT_HEAD_DIM
N_PREV_CHUNKS = 8
REL_CLIP = 256
SSD_HEAD_DIM = 64
SSD_WIDTH = D_MODEL
SSD_HEADS = SSD_WIDTH // SSD_HEAD_DIM
SSD_GROUPS = 2
SSD_STATE = 128
SSD_CONV = 4
CONV_DIM = SSD_WIDTH + 2 * SSD_GROUPS * SSD_STATE
MAIN_PROJ = 3 * ATT_WIDTH + SSD_WIDTH + CONV_DIM
D_FF = 4 * D_MODEL
EPS = 1e-5
NEG = -1e30

LANES = 128
SUBLANES = 8

PROJ_TM = 1024
ATT_TQ = 256
ATT_BAND = ATT_TQ + N_PREV_CHUNKS * CHUNK
SSD_L = 128
SSD_T = 512
MIX_PAIRS_PER_ITEM = 2
FFN_TM = 512
VMEM_LIMIT = 56 * 1024 * 1024


def _rms(x, w):
    ms = jnp.mean(x * x, axis=-1, keepdims=True)
    return x * lax.rsqrt(ms + EPS) * w


def _silu(x):
    h = 0.5 * x
    return h + h * jnp.tanh(h)


def _in_proj_kernel(x_ref, nw_ref, wm_ref, wkt_ref,
                    q_ref, kt_ref, v_ref, gz_ref, xbc_ref, dtt_ref):
    hn = _rms(x_ref[...], nw_ref[...]).astype(BF16)

    def seg(lo, hi):
        return jnp.dot(hn, wm_ref[:, lo:hi], preferred_element_type=F32).astype(BF16)

    a = ATT_WIDTH
    q_ref[...] = seg(0, a)
    ktd = lax.dot_general(wkt_ref[...], hn, (((1,), (1,)), ((), ())),
                          preferred_element_type=F32)
    kt_ref[...] = ktd[:a, :].astype(BF16)
    dtt_ref[...] = ktd[a:, :]
    v_ref[...] = seg(a, 2 * a)
    gz_ref[...] = _silu(jnp.dot(hn, wm_ref[:, 2 * a:2 * a + SSD_WIDTH],
                                preferred_element_type=F32)).astype(BF16)
    xbc_ref[...] = seg(2 * a + SSD_WIDTH, MAIN_PROJ - a)


def _in_proj(x2, norm_w, w_main, w_ktd):
    m = x2.shape[0]
    tm = PROJ_TM
    row = lambda i: (i, 0)
    col = lambda i: (0, i)
    const = lambda i: (0, 0)
    once = pl.Buffered(1)
    out_shape = (
        jax.ShapeDtypeStruct((m, ATT_WIDTH), BF16),
        jax.ShapeDtypeStruct((ATT_WIDTH, m), BF16),
        jax.ShapeDtypeStruct((m, ATT_WIDTH), BF16),
        jax.ShapeDtypeStruct((m, SSD_WIDTH), BF16),
        jax.ShapeDtypeStruct((m, CONV_DIM), BF16),
        jax.ShapeDtypeStruct((SSD_HEADS, m), F32),
    )
    return pl.pallas_call(
        _in_proj_kernel,
        out_shape=out_shape,
        grid=(m // tm,),
        in_specs=[
            pl.BlockSpec((tm, D_MODEL), row),
            pl.BlockSpec((1, D_MODEL), const),
            pl.BlockSpec((D_MODEL, MAIN_PROJ - ATT_WIDTH), const, pipeline_mode=once),
            pl.BlockSpec((ATT_WIDTH + SSD_HEADS, D_MODEL), const, pipeline_mode=once),
        ],
        out_specs=(
            pl.BlockSpec((tm, ATT_WIDTH), row),
            pl.BlockSpec((ATT_WIDTH, tm), col),
            pl.BlockSpec((tm, ATT_WIDTH), row),
            pl.BlockSpec((tm, SSD_WIDTH), row),
            pl.BlockSpec((tm, CONV_DIM), row),
            pl.BlockSpec((SSD_HEADS, tm), col),
        ),
        compiler_params=pltpu.CompilerParams(
            dimension_semantics=("parallel",), vmem_limit_bytes=VMEM_LIMIT),
        name="in_proj",
    )(x2, norm_w, w_main, w_ktd)


ATT_SLOTS = ATT_BAND // ATT_TQ
ATT_NB = 2
ATT_RING = ATT_NB + ATT_SLOTS - 1
ATT_AHEAD = 1
PAIR = 2 * ATT_HEAD_DIM
LOG2E = 1.4426950408889634


def _bias_table_kernel(r_ref, o_ref):
    x = jnp.broadcast_to(r_ref[0], (ATT_TQ, ATT_BAND + ATT_TQ))
    rolled = pltpu.roll(x, 0, 1, stride=1, stride_axis=0)
    t = rolled[:, ATT_TQ:]
    qc = lax.broadcasted_iota(jnp.int32, (ATT_TQ, ATT_BAND), 0) // CHUNK
    kc = lax.broadcasted_iota(jnp.int32, (ATT_TQ, ATT_BAND), 1) // CHUNK
    valid = (kc >= qc) & (kc <= qc + N_PREV_CHUNKS)
    valid = valid & (pl.program_id(0) < ATT_HEADS)
    o_ref[0] = jnp.where(valid, t * LOG2E, NEG)


def _bias_table(rel_bias):
    rb = rel_bias.astype(F32).T
    n = ATT_BAND + ATT_TQ
    far = jnp.broadcast_to(rb[:, 2 * REL_CLIP:], (ATT_HEADS, n - 2 * REL_CLIP + 1))
    near = rb[:, 2 * REL_CLIP - 1:0:-1]
    r = jnp.concatenate([far, near], axis=1)
    r = jnp.concatenate([r, jnp.zeros((1, n), F32)], axis=0).reshape(ATT_HEADS + 1, 1, n)
    return pl.pallas_call(
        _bias_table_kernel,
        out_shape=jax.ShapeDtypeStruct((ATT_HEADS + 1, ATT_TQ, ATT_BAND), F32),
        grid=(ATT_HEADS + 1,),
        in_specs=[pl.BlockSpec((1, 1, n), lambda h: (h, 0, 0))],
        out_specs=pl.BlockSpec((1, ATT_TQ, ATT_BAND), lambda h: (h, 0, 0)),
        compiler_params=pltpu.CompilerParams(dimension_semantics=("parallel",)),
        name="bias_table",
    )(r)


def _attn_steps(i, q_ref, kt_ref, v_ref, tab_ref, o_ref, kring, vring):
    tq = ATT_TQ
    hd = ATT_HEAD_DIM
    lane = lax.broadcasted_iota(jnp.int32, (tq, PAIR), 1)
    low = lane < hd

    @pl.when(i == 0)
    def _():
        kring[...] = jnp.zeros(kring.shape, BF16)
        vring[...] = jnp.zeros(vring.shape, BF16)

    for j in range(ATT_NB):
        slot_new = lax.rem(ATT_NB * i + j, ATT_RING)
        cols = slice(j * tq, (j + 1) * tq)
        for h in range(ATT_HEADS):
            r0 = h * PAIR + (h % 2) * hd
            kring[slot_new, r0:r0 + hd, :] = kt_ref[h * hd:(h + 1) * hd, cols]
        for p in range(ATT_HEADS // 2):
            vp = v_ref[0, cols, p * PAIR:(p + 1) * PAIR]
            one = jnp.ones_like(vp)
            vring[slot_new, :, (2 * p) * PAIR:(2 * p + 1) * PAIR] = jnp.where(low, vp, one)
            vring[slot_new, :, (2 * p + 1) * PAIR:(2 * p + 2) * PAIR] = jnp.where(low, one, vp)

    def band(j):
        n = [ATT_NB * i + j - (ATT_SLOTS - 1) + m for m in range(ATT_SLOTS)]
        slots = [lax.rem(nm + ATT_RING, ATT_RING) for nm in n]
        masked = [nm < 0 for nm in n]
        return slots, masked

    bands = [band(j) for j in range(ATT_NB)]

    def scores(j, h):
        slots, masked = bands[j]
        p = h // 2
        qp = q_ref[0, j * tq:(j + 1) * tq, p * PAIR:(p + 1) * PAIR]
        return [jnp.dot(qp, kring[slots[m], h * PAIR:(h + 1) * PAIR, :],
                        preferred_element_type=F32)
                + tab_ref[jnp.where(masked[m], ATT_HEADS, h), :, m * tq:(m + 1) * tq]
                for m in range(ATT_SLOTS)]

    items = [(j, h) for j in range(ATT_NB) for h in range(ATT_HEADS)]
    queue = [scores(*it) for it in items[:ATT_AHEAD]]
    res = []
    yield
    for idx, (j, h) in enumerate(items):
        s = queue.pop(0)
        if idx + ATT_AHEAD < len(items):
            queue.append(scores(*items[idx + ATT_AHEAD]))
        slots, _ = bands[j]
        mx = functools.reduce(jnp.maximum, s)
        mx = jnp.max(mx, axis=-1, keepdims=True)
        acc = None
        for m in range(ATT_SLOTS):
            e = jnp.exp2(s[m] - mx).astype(BF16)
            pv = jnp.dot(e, vring[slots[m], :, h * PAIR:(h + 1) * PAIR],
                         preferred_element_type=F32)
            acc = pv if acc is None else acc + pv
        res.append(acc)
        if h % 2 == 1:
            p = h // 2
            num = jnp.where(low, res[0], res[1])
            den = jnp.where(low, pltpu.roll(res[0], hd, 1), pltpu.roll(res[1], hd, 1))
            o_ref[0, j * tq:(j + 1) * tq, p * PAIR:(p + 1) * PAIR] = (num / den).astype(BF16)
            res = []
        yield


def _split3(v):
    v1 = v.astype(BF16).astype(F32)
    r1 = v - v1
    v2 = r1.astype(BF16).astype(F32)
    v3 = (r1 - v2).astype(BF16).astype(F32)
    return v1, v2, v3


def _expand_heads(v, e3_ref):
    lane = lax.broadcasted_iota(jnp.int32, v.shape, 1)
    v = jnp.where(lane < SSD_HEADS, v, 0.0)
    v1, v2, v3 = _split3(v)
    packed = v1 + pltpu.roll(v2, SSD_HEADS, 1) + pltpu.roll(v3, 2 * SSD_HEADS, 1)
    return jnp.dot(packed.astype(BF16), e3_ref[...], preferred_element_type=F32)


CONV_TILE = 256
N_CONV_TILES = CONV_DIM // CONV_TILE


def _conv_tile(c, k, cw_ref, cb_ref, shift_ref, ubuf):
    L = SSD_L
    ch = slice(k * CONV_TILE, (k + 1) * CONV_TILE)
    win = ubuf[c * L:c * L + 2 * L, ch]
    sh = jnp.dot(shift_ref[...], win, preferred_element_type=F32)
    acc = cb_ref[:, ch] + cw_ref[SSD_CONV - 1:SSD_CONV, ch] * win[L:, :].astype(F32)
    for d in range(1, SSD_CONV):
        acc = acc + cw_ref[SSD_CONV - 1 - d:SSD_CONV - d, ch] * sh[(d - 1) * L:d * L, :]
    return _silu(acc)


def _decay_cumsum(c, dtt_ref, dtb_ref, alog_ref, tril3_ref):
    L = SSD_L
    rows = slice(c * L, (c + 1) * L)
    raw_t = jnp.concatenate([dtt_ref[:, rows], jnp.zeros((LANES - SSD_HEADS, L), F32)], axis=0)
    u = raw_t.T + dtb_ref[...]
    dt = jnp.maximum(u, 0.0) + jnp.log1p(jnp.exp(-jnp.abs(u)))
    a = dt * (-jnp.exp(alog_ref[...]) * LOG2E)
    a1, a2, a3 = _split3(a)
    a_parts = jnp.concatenate([a1, a2, a3], axis=0).astype(BF16)
    a_cum = jnp.dot(tril3_ref[...], a_parts, preferred_element_type=F32)
    return dt, a_cum


def _decay_weights(dt, a_cum, e3_ref):
    L = SSD_L
    a_last = a_cum[L - 1:L, :]
    w = dt * jnp.exp2(a_last - a_cum)
    w_exp = _expand_heads(w, e3_ref)
    cd = _expand_heads(jnp.broadcast_to(jnp.exp2(a_last), (SUBLANES, LANES)), e3_ref)[0:1, :]
    return w_exp, cd


def _ssd_front(tiles, dt, a_cum, w_exp, cd):
    xc = jnp.concatenate(tiles, axis=1)
    xs = xc[:, :SSD_WIDTH]
    xs_bf = xs.astype(BF16)
    bm = xc[:, SSD_WIDTH:SSD_WIDTH + SSD_GROUPS * SSD_STATE]
    cm = xc[:, SSD_WIDTH + SSD_GROUPS * SSD_STATE:]
    xd = (xs * w_exp).astype(BF16)
    return xs, xs_bf, bm, cm, a_cum, a_cum.T, dt.T, jnp.exp2(a_cum), xd, cd


def _ssd_back(c, front, gz_ref, dexp_ref, y_ref, state, causal, low, hooks, tick):
    L = SSD_L
    gw = SSD_WIDTH // SSD_GROUPS
    ppg = SSD_HEADS // SSD_GROUPS // 2
    rows = slice(c * L, (c + 1) * L)
    xs, xs_bf, bm, cm, a_cum, a_cum_t, dt_t, exp_acum, xd, cd = front

    def finish(p, lhs, rhs):
        yp = jnp.dot(lhs, rhs, preferred_element_type=F32)
        yp = yp + dexp_ref[:, p * PAIR:(p + 1) * PAIR] * xs[:, p * PAIR:(p + 1) * PAIR]
        gz = gz_ref[0, rows, p * PAIR:(p + 1) * PAIR].astype(F32)
        y_ref[0, rows, p * PAIR:(p + 1) * PAIR] = (yp * gz).astype(BF16)

    zero = jnp.zeros((L, PAIR), BF16)
    pending = None
    for g in range(SSD_GROUPS):
        bg = bm[:, g * SSD_STATE:(g + 1) * SSD_STATE]
        cg = cm[:, g * SSD_STATE:(g + 1) * SSD_STATE]
        cb = lax.dot_general(cg.astype(BF16), bg.astype(BF16), (((1,), (1,)), ((), ())),
                             preferred_element_type=F32)
        sg = state[g]
        sg_bf = sg.astype(BF16)
        for q in range(ppg):
            p = g * ppg + q
            lhs = []
            for h in (2 * p, 2 * p + 1):
                seg = a_cum[:, h:h + 1] - a_cum_t[h:h + 1, :]
                dec = jnp.where(causal, jnp.exp2(seg), 0.0)
                lhs.append((cb * dec * dt_t[h:h + 1, :]).astype(BF16))
            for h in (2 * p, 2 * p + 1):
                lhs.append((cg * exp_acum[:, h:h + 1]).astype(BF16))
            xp = xs_bf[:, p * PAIR:(p + 1) * PAIR]
            sp = sg_bf[:, q * PAIR:(q + 1) * PAIR]
            rhs = jnp.concatenate([jnp.where(low, xp, zero), jnp.where(low, zero, xp),
                                   jnp.where(low, sp, zero), jnp.where(low, zero, sp)], axis=0)
            lhs = jnp.concatenate(lhs, axis=1)
            if pending is not None:
                finish(*pending)
            pending = (p, lhs, rhs)
            if p < len(hooks):
                hooks[p]()
            tick()
        finish(*pending)
        pending = None
        s_new = jnp.dot(bg.T.astype(BF16), xd[:, g * gw:(g + 1) * gw],
                        preferred_element_type=F32)
        state[g] = sg * cd[:, g * gw:(g + 1) * gw] + s_new


def _ssd_steps(j, gz_ref, xbc_ref, dtt_ref, cw_ref, cb_ref, dtb_ref, alog_ref,
               dexp_ref, e3_ref, tril3_ref, shift_ref, y_ref, ubuf, state, tick):
    T = SSD_T
    L = SSD_L

    @pl.when(j == 0)
    def _():
        ubuf[0:L, :] = jnp.zeros((L, CONV_DIM), BF16)
        state[...] = jnp.zeros(state.shape, F32)

    @pl.when(j > 0)
    def _():
        ubuf[0:L, :] = ubuf[T:T + L, :]

    ubuf[L:T + L, :] = xbc_ref[0]

    row = lax.broadcasted_iota(jnp.int32, (L, L), 0)
    col = lax.broadcasted_iota(jnp.int32, (L, L), 1)
    causal = row >= col
    low = lax.broadcasted_iota(jnp.int32, (L, PAIR), 1) < SSD_HEAD_DIM
    def conv(c, k):
        return _conv_tile(c, k, cw_ref, cb_ref, shift_ref, ubuf)

    def cumsum(c):
        return _decay_cumsum(c, dtt_ref, dtb_ref, alog_ref, tril3_ref)

    n = T // L
    dt, a_cum = cumsum(0)
    tiles = [conv(0, k) for k in range(N_CONV_TILES)]
    cur = _ssd_front(tiles, dt, a_cum, *_decay_weights(dt, a_cum, e3_ref))
    for c in range(n):
        nxt = {}
        hooks = []
        if c + 1 < n:
            def h_cumsum(cc=c + 1):
                nxt["dt"], nxt["a_cum"] = cumsum(cc)

            def h_conv(k, cc=c + 1):
                nxt.setdefault("tiles", []).append(conv(cc, k))

            def h_weights():
                nxt["w"] = _decay_weights(nxt["dt"], nxt["a_cum"], e3_ref)

            hooks = ([h_cumsum] + [functools.partial(h_conv, k) for k in range(N_CONV_TILES)]
                     + [h_weights])
        _ssd_back(c, cur, gz_ref, dexp_ref, y_ref, state, causal, low, hooks, tick)
        if c + 1 < n:
            cur = _ssd_front(nxt["tiles"], nxt["dt"], nxt["a_cum"], *nxt["w"])


def _mix_kernel(q_ref, kt_ref, v_ref, tab_ref, gz_ref, xbc_ref, dtt_ref, cw_ref, cb_ref,
                dtb_ref, alog_ref, dexp_ref, e3_ref, tril3_ref, shift_ref,
                att_ref, y_ref, kring, vring, ubuf, state):
    i = pl.program_id(1)
    att = _attn_steps(i, q_ref, kt_ref, v_ref, tab_ref, att_ref, kring, vring)
    pairs_done = [0]

    def tick():
        pairs_done[0] += 1
        if pairs_done[0] % MIX_PAIRS_PER_ITEM == 0:
            next(att, None)

    next(att)
    _ssd_steps(i, gz_ref, xbc_ref, dtt_ref, cw_ref, cb_ref, dtb_ref, alog_ref, dexp_ref,
               e3_ref, tril3_ref, shift_ref, y_ref, ubuf, state, tick)
    for _ in att:
        pass


def _mixers(q, kt, v, table, gz, xbc, dtt_raw, conv_w, conv_b, dt_bias_p, a_log_p, d_exp,
            e3, tril3, shift):
    b, s, _ = gz.shape
    T = SSD_T
    assert T == ATT_NB * ATT_TQ
    nblk = s // T
    blk = lambda bi, j: (bi, j, 0)
    colblk = lambda bi, j: (0, bi * nblk + j)
    const = lambda bi, j: (0, 0)
    once = pl.Buffered(1)
    return pl.pallas_call(
        _mix_kernel,
        out_shape=(jax.ShapeDtypeStruct((b, s, ATT_WIDTH), BF16),
                   jax.ShapeDtypeStruct((b, s, SSD_WIDTH), BF16)),
        grid=(b, nblk),
        in_specs=[
            pl.BlockSpec((1, T, ATT_WIDTH), blk),
            pl.BlockSpec((ATT_WIDTH, T), colblk),
            pl.BlockSpec((1, T, ATT_WIDTH), blk),
            pl.BlockSpec((ATT_HEADS + 1, ATT_TQ, ATT_BAND), lambda bi, j: (0, 0, 0),
                         pipeline_mode=once),
            pl.BlockSpec((1, T, SSD_WIDTH), blk),
            pl.BlockSpec((1, T, CONV_DIM), blk),
            pl.BlockSpec((SSD_HEADS, T), colblk),
            pl.BlockSpec((SSD_CONV, CONV_DIM), const),
            pl.BlockSpec((1, CONV_DIM), const),
            pl.BlockSpec((1, LANES), const),
            pl.BlockSpec((1, LANES), const),
            pl.BlockSpec((1, SSD_WIDTH), const),
            pl.BlockSpec((LANES, SSD_WIDTH), const, pipeline_mode=once),
            pl.BlockSpec((SSD_L, 3 * SSD_L), const, pipeline_mode=once),
            pl.BlockSpec(((SSD_CONV - 1) * SSD_L, 2 * SSD_L), const, pipeline_mode=once),
        ],
        out_specs=(pl.BlockSpec((1, T, ATT_WIDTH), blk),
                   pl.BlockSpec((1, T, SSD_WIDTH), blk)),
        scratch_shapes=[
            pltpu.VMEM((ATT_RING, ATT_HEADS * PAIR, ATT_TQ), BF16),
            pltpu.VMEM((ATT_RING, ATT_TQ, ATT_HEADS * PAIR), BF16),
            pltpu.VMEM((T + SSD_L, CONV_DIM), BF16),
            pltpu.VMEM((SSD_GROUPS, SSD_STATE, SSD_WIDTH // SSD_GROUPS), F32),
        ],
        compiler_params=pltpu.CompilerParams(
            dimension_semantics=("parallel", "arbitrary"),
            vmem_limit_bytes=VMEM_LIMIT),
        name="mixers",
    )(q, kt, v, table, gz, xbc, dtt_raw, conv_w, conv_b, dt_bias_p, a_log_p, d_exp,
      e3, tril3, shift)


def _shift_matrix():
    r = jnp.arange((SSD_CONV - 1) * SSD_L)[:, None]
    c = jnp.arange(2 * SSD_L)[None, :]
    return (c == SSD_L + r % SSD_L - (r // SSD_L + 1)).astype(BF16)


def _tril3_matrix():
    r = jnp.arange(SSD_L)[:, None]
    c = jnp.arange(3 * SSD_L)[None, :] % SSD_L
    return (r >= c).astype(BF16)


def _expand_matrix():
    r = jnp.arange(LANES)[:, None]
    c = jnp.arange(SSD_WIDTH)[None, :]
    hit = (r < 3 * SSD_HEADS) & ((r % SSD_HEADS) == c // SSD_HEAD_DIM)
    return hit.astype(BF16)


def _out_ffn_kernel(x_ref, att_ref, ssd_ref, sw_ref, woa_ref, wos_ref, nw_ref,
                    w1_ref, w2_ref, fw_ref, o_ref):
    gw = SSD_WIDTH // SSD_GROUPS
    ssd = []
    for g in range(SSD_GROUPS):
        yg = ssd_ref[:, g * gw:(g + 1) * gw].astype(F32)
        ssd.append(_rms(yg, sw_ref[:, g * gw:(g + 1) * gw]).astype(BF16))
    ssd = jnp.concatenate(ssd, axis=1)
    h = (x_ref[...]
         + jnp.dot(att_ref[...], woa_ref[...], preferred_element_type=F32)
         + jnp.dot(ssd, wos_ref[...], preferred_element_type=F32))
    hn = _rms(h, nw_ref[...]).astype(BF16)
    hid = jnp.dot(hn, w1_ref[...], preferred_element_type=F32)
    hid = jnp.square(jnp.maximum(hid, 0.0)).astype(BF16)
    h = h + jnp.dot(hid, w2_ref[...], preferred_element_type=F32)
    o_ref[...] = _rms(h, fw_ref[...])


def _out_ffn(x2, att2, ssd2, ssd_norm_w, wo_a, wo_s, norm_w, w1, w2, final_w):
    m = x2.shape[0]
    tm = FFN_TM
    row = lambda i: (i, 0)
    const = lambda i: (0, 0)
    once = pl.Buffered(1)
    return pl.pallas_call(
        _out_ffn_kernel,
        out_shape=jax.ShapeDtypeStruct((m, D_MODEL), F32),
        grid=(m // tm,),
        in_specs=[
            pl.BlockSpec((tm, D_MODEL), row),
            pl.BlockSpec((tm, ATT_WIDTH), row),
            pl.BlockSpec((tm, SSD_WIDTH), row),
            pl.BlockSpec((1, SSD_WIDTH), const),
            pl.BlockSpec((ATT_WIDTH, D_MODEL), const, pipeline_mode=once),
            pl.BlockSpec((SSD_WIDTH, D_MODEL), const, pipeline_mode=once),
            pl.BlockSpec((1, D_MODEL), const),
            pl.BlockSpec((D_MODEL, D_FF), const, pipeline_mode=once),
            pl.BlockSpec((D_FF, D_MODEL), const, pipeline_mode=once),
            pl.BlockSpec((1, D_MODEL), const),
        ],
        out_specs=pl.BlockSpec((tm, D_MODEL), row),
        compiler_params=pltpu.CompilerParams(
            dimension_semantics=("parallel",), vmem_limit_bytes=VMEM_LIMIT),
        name="out_ffn",
    )(x2, att2, ssd2, ssd_norm_w, wo_a, wo_s, norm_w, w1, w2, final_w)


def _layer(x, norm_mix_w, w_in, rel_bias, conv_w, conv_b, dt_bias, a_log, d_skip,
           ssd_norm_w, w_out, norm_mlp_w, w_ff1, w_ff2, out_norm_w):
    b, s, d = x.shape
    m = b * s
    x2 = x.reshape(m, d)

    scale = ATT_HEAD_DIM ** -0.5 * LOG2E
    w_main = jnp.concatenate(
        [w_in[:, :ATT_WIDTH] * scale, w_in[:, 2 * ATT_WIDTH:MAIN_PROJ]], axis=1).astype(BF16)
    w_ktd = jnp.concatenate([w_in[:, ATT_WIDTH:2 * ATT_WIDTH], w_in[:, MAIN_PROJ:]],
                            axis=1).T.astype(BF16)
    pad_h = lambda v: jnp.pad(v.astype(F32), (0, LANES - SSD_HEADS)).reshape(1, LANES)

    q, kt, v, gz, xbc, dtt_raw = _in_proj(x2, norm_mix_w.reshape(1, d), w_main, w_ktd)

    att, ssd = _mixers(q.reshape(b, s, -1), kt, v.reshape(b, s, -1), _bias_table(rel_bias),
                       gz.reshape(b, s, -1), xbc.reshape(b, s, -1), dtt_raw,
                       conv_w.astype(F32), conv_b.reshape(1, -1).astype(F32),
                       pad_h(dt_bias), pad_h(a_log),
                       jnp.repeat(d_skip.astype(F32), SSD_HEAD_DIM).reshape(1, -1),
                       _expand_matrix(), _tril3_matrix(), _shift_matrix())

    y = _out_ffn(x2, att.reshape(m, -1), ssd.reshape(m, -1),
                 ssd_norm_w.reshape(1, -1).astype(F32),
                 w_out[:ATT_WIDTH].astype(BF16), w_out[ATT_WIDTH:].astype(BF16),
                 norm_mlp_w.reshape(1, d), w_ff1.astype(BF16), w_ff2.astype(BF16),
                 out_norm_w.reshape(1, d))
    return y.reshape(b, s, d)


def kernel(x, norm_mix_w, w_in, rel_bias, conv_w, conv_b, dt_bias, a_log, d_skip,
           ssd_norm_w, w_out, norm_mlp_w, w_ff1, w_ff2, norm_final_w):
    depth = w_in.shape[0]
    assert depth == 1, "final norm is fused into the single layer's last kernel"
    return _layer(x, norm_mix_w[0], w_in[0], rel_bias[0], conv_w[0], conv_b[0],
                  dt_bias[0], a_log[0], d_skip[0], ssd_norm_w[0], w_out[0],
                  norm_mlp_w[0], w_ff1[0], w_ff2[0], norm_final_w)
```

```python
import functools

import jax
import jax.numpy as jnp
from jax import lax
from jax.experimental import pallas as pl
from jax.experimental.pallas import tpu as pltpu

F32 = jnp.float32
BF16 = jnp.bfloat16

D_MODEL = 1024
CHUNK = 64
ATT_HEADS = 8
ATT_HEAD_DIM = 64
ATT_WIDTH = ATT_HEADS * ATT_HEAD_DIM
N_PREV_CHUNKS = 8
REL_CLIP = 256
SSD_HEAD_DIM = 64
SSD_WIDTH = D_MODEL
SSD_HEADS = SSD_WIDTH // SSD_HEAD_DIM
SSD_GROUPS = 2
SSD_STATE = 128
SSD_CONV = 4
CONV_DIM = SSD_WIDTH + 2 * SSD_GROUPS * SSD_STATE
MAIN_PROJ = 3 * ATT_WIDTH + SSD_WIDTH + CONV_DIM
D_FF = 4 * D_MODEL
EPS = 1e-5
NEG = -1e30

LANES = 128
SUBLANES = 8

PROJ_TM = 1024
ATT_TQ = 256
ATT_BAND = ATT_TQ + N_PREV_CHUNKS * CHUNK
SSD_L = 128
SSD_T = 512
MIX_PAIRS_PER_ITEM = 1
FFN_TM = 1024
VMEM_LIMIT = 56 * 1024 * 1024


def _rms(x, w):
    ms = jnp.mean(x * x, axis=-1, keepdims=True)
    return x * lax.rsqrt(ms + EPS) * w


def _silu(x):
    h = 0.5 * x
    return h + h * jnp.tanh(h)


def _in_proj_kernel(x_ref, nw_ref, wm_ref, wkt_ref,
                    q_ref, kt_ref, v_ref, gz_ref, xbc_ref, dtt_ref):
    hn = _rms(x_ref[...], nw_ref[...]).astype(BF16)

    def seg(lo, hi):
        return jnp.dot(hn, wm_ref[:, lo:hi], preferred_element_type=F32).astype(BF16)

    a = ATT_WIDTH
    zlo = 3 * a
    xlo = zlo + SSD_WIDTH
    q_ref[...] = seg(0, a)
    ktd = lax.dot_general(wkt_ref[...], hn, (((1,), (1,)), ((), ())),
                          preferred_element_type=F32)
    kt_ref[...] = ktd[:a, :].astype(BF16)
    dtt_ref[...] = ktd[a:, :]
    v_ref[...] = seg(2 * a, 3 * a)
    gz_ref[...] = _silu(jnp.dot(hn, wm_ref[:, zlo:xlo],
                                preferred_element_type=F32)).astype(BF16)
    xbc_ref[...] = seg(xlo, MAIN_PROJ)


def _in_proj(x2, norm_w, w_main, w_ktd):
    m = x2.shape[0]
    tm = PROJ_TM
    row = lambda i: (i, 0)
    col = lambda i: (0, i)
    const = lambda i: (0, 0)
    once = pl.Buffered(1)
    out_shape = (
        jax.ShapeDtypeStruct((m, ATT_WIDTH), BF16),
        jax.ShapeDtypeStruct((ATT_WIDTH, m), BF16),
        jax.ShapeDtypeStruct((m, ATT_WIDTH), BF16),
        jax.ShapeDtypeStruct((m, SSD_WIDTH), BF16),
        jax.ShapeDtypeStruct((m, CONV_DIM), BF16),
        jax.ShapeDtypeStruct((SSD_HEADS, m), F32),
    )
    return pl.pallas_call(
        _in_proj_kernel,
        out_shape=out_shape,
        grid=(m // tm,),
        in_specs=[
            pl.BlockSpec((tm, D_MODEL), row),
            pl.BlockSpec((1, D_MODEL), const),
            pl.BlockSpec((D_MODEL, MAIN_PROJ + SSD_HEADS), const, pipeline_mode=once),
            pl.BlockSpec((ATT_WIDTH + SSD_HEADS, D_MODEL), const, pipeline_mode=once),
        ],
        out_specs=(
            pl.BlockSpec((tm, ATT_WIDTH), row),
            pl.BlockSpec((ATT_WIDTH, tm), col),
            pl.BlockSpec((tm, ATT_WIDTH), row),
            pl.BlockSpec((tm, SSD_WIDTH), row),
            pl.BlockSpec((tm, CONV_DIM), row),
            pl.BlockSpec((SSD_HEADS, tm), col),
        ),
        compiler_params=pltpu.CompilerParams(
            dimension_semantics=("parallel",), vmem_limit_bytes=VMEM_LIMIT),
        name="in_proj",
    )(x2, norm_w, w_main, w_ktd)


ATT_SLOTS = ATT_BAND // ATT_TQ
ATT_NB = 2
ATT_RING = ATT_NB + ATT_SLOTS - 1
ATT_AHEAD = 2
PAIR = 2 * ATT_HEAD_DIM
LOG2E = 1.4426950408889634


def _bias_table_kernel(r_ref, o_ref):
    x = jnp.broadcast_to(r_ref[0], (ATT_TQ, ATT_BAND + ATT_TQ))
    rolled = pltpu.roll(x, 0, 1, stride=1, stride_axis=0)
    t = rolled[:, ATT_TQ:]
    qc = lax.broadcasted_iota(jnp.int32, (ATT_TQ, ATT_BAND), 0) // CHUNK
    kc = lax.broadcasted_iota(jnp.int32, (ATT_TQ, ATT_BAND), 1) // CHUNK
    valid = (kc >= qc) & (kc <= qc + N_PREV_CHUNKS)
    valid = valid & (pl.program_id(0) < ATT_HEADS)
    o_ref[0] = jnp.where(valid, t * LOG2E, NEG)


def _bias_table(rel_bias):
    rb = rel_bias.astype(F32).T
    n = ATT_BAND + ATT_TQ
    far = jnp.broadcast_to(rb[:, 2 * REL_CLIP:], (ATT_HEADS, n - 2 * REL_CLIP + 1))
    near = rb[:, 2 * REL_CLIP - 1:0:-1]
    r = jnp.concatenate([far, near], axis=1)
    r = jnp.concatenate([r, jnp.zeros((1, n), F32)], axis=0).reshape(ATT_HEADS + 1, 1, n)
    return pl.pallas_call(
        _bias_table_kernel,
        out_shape=jax.ShapeDtypeStruct((ATT_HEADS + 1, ATT_TQ, ATT_BAND), F32),
        grid=(ATT_HEADS + 1,),
        in_specs=[pl.BlockSpec((1, 1, n), lambda h: (h, 0, 0))],
        out_specs=pl.BlockSpec((1, ATT_TQ, ATT_BAND), lambda h: (h, 0, 0)),
        compiler_params=pltpu.CompilerParams(dimension_semantics=("parallel",)),
        name="bias_table",
    )(r)


def _attn_steps(i, q_ref, kt_ref, v_ref, tab_ref, o_ref, kring, vring):
    tq = ATT_TQ
    hd = ATT_HEAD_DIM
    lane = lax.broadcasted_iota(jnp.int32, (tq, PAIR), 1)
    low = lane < hd

    @pl.when(i == 0)
    def _():
        kring[...] = jnp.zeros(kring.shape, BF16)
        vring[...] = jnp.zeros(vring.shape, BF16)

    for j in range(ATT_NB):
        slot_new = lax.rem(ATT_NB * i + j, ATT_RING)
        cols = slice(j * tq, (j + 1) * tq)
        for h in range(ATT_HEADS):
            r0 = h * PAIR + (h % 2) * hd
            kring[slot_new, r0:r0 + hd, :] = kt_ref[h * hd:(h + 1) * hd, cols]
        for p in range(ATT_HEADS // 2):
            vp = v_ref[0, cols, p * PAIR:(p + 1) * PAIR]
            one = jnp.ones_like(vp)
            vring[slot_new, :, (2 * p) * PAIR:(2 * p + 1) * PAIR] = jnp.where(low, vp, one)
            vring[slot_new, :, (2 * p + 1) * PAIR:(2 * p + 2) * PAIR] = jnp.where(low, one, vp)

    def band(j):
        n = [ATT_NB * i + j - (ATT_SLOTS - 1) + m for m in range(ATT_SLOTS)]
        slots = [lax.rem(nm + ATT_RING, ATT_RING) for nm in n]
        masked = [nm < 0 for nm in n]
        return slots, masked

    bands = [band(j) for j in range(ATT_NB)]

    def scores(j, h):
        slots, masked = bands[j]
        p = h // 2
        qp = q_ref[0, j * tq:(j + 1) * tq, p * PAIR:(p + 1) * PAIR]
        return [jnp.dot(qp, kring[slots[m], h * PAIR:(h + 1) * PAIR, :],
                        preferred_element_type=F32)
                + tab_ref[jnp.where(masked[m], ATT_HEADS, h), :, m * tq:(m + 1) * tq]
                for m in range(ATT_SLOTS)]

    items = [(j, h) for j in range(ATT_NB) for h in range(ATT_HEADS)]
    queue = [scores(*it) for it in items[:ATT_AHEAD]]
    res = []
    yield
    for idx, (j, h) in enumerate(items):
        s = queue.pop(0)
        if idx + ATT_AHEAD < len(items):
            queue.append(scores(*items[idx + ATT_AHEAD]))
        slots, _ = bands[j]
        mx = functools.reduce(jnp.maximum, s)
        mx = jnp.max(mx, axis=-1, keepdims=True)
        acc = None
        for m in range(ATT_SLOTS):
            e = jnp.exp2(s[m] - mx).astype(BF16)
            pv = jnp.dot(e, vring[slots[m], :, h * PAIR:(h + 1) * PAIR],
                         preferred_element_type=F32)
            acc = pv if acc is None else acc + pv
        res.append(acc)
        if h % 2 == 1:
            p = h // 2
            num = jnp.where(low, res[0], res[1])
            den = jnp.where(low, pltpu.roll(res[0], hd, 1), pltpu.roll(res[1], hd, 1))
            o_ref[0, j * tq:(j + 1) * tq, p * PAIR:(p + 1) * PAIR] = (num / den).astype(BF16)
            res = []
        yield


def _split3(v):
    v1 = v.astype(BF16).astype(F32)
    r1 = v - v1
    v2 = r1.astype(BF16).astype(F32)
    v3 = (r1 - v2).astype(BF16).astype(F32)
    return v1, v2, v3


def _expand_heads(v, e3_ref):
    lane = lax.broadcasted_iota(jnp.int32, v.shape, 1)
    v = jnp.where(lane < SSD_HEADS, v, 0.0)
    v1, v2, v3 = _split3(v)
    packed = v1 + pltpu.roll(v2, SSD_HEADS, 1) + pltpu.roll(v3, 2 * SSD_HEADS, 1)
    return jnp.dot(packed.astype(BF16), e3_ref[...], preferred_element_type=F32)


CONV_TILE = 256
N_CONV_TILES = CONV_DIM // CONV_TILE


def _conv_tile(c, k, cw_ref, cb_ref, shift_ref, ubuf):
    L = SSD_L
    ch = slice(k * CONV_TILE, (k + 1) * CONV_TILE)
    win = ubuf[c * L:c * L + 2 * L, ch]
    sh = jnp.dot(shift_ref[...], win, preferred_element_type=F32)
    acc = cb_ref[:, ch] + cw_ref[SSD_CONV - 1:SSD_CONV, ch] * win[L:, :].astype(F32)
    for d in range(1, SSD_CONV):
        acc = acc + cw_ref[SSD_CONV - 1 - d:SSD_CONV - d, ch] * sh[(d - 1) * L:d * L, :]
    return _silu(acc)


def _decay_cumsum(c, dtt_ref, dtb_ref, alog_ref, tril3_ref):
    L = SSD_L
    rows = slice(c * L, (c + 1) * L)
    raw_t = jnp.concatenate([dtt_ref[:, rows], jnp.zeros((LANES - SSD_HEADS, L), F32)], axis=0)
    u = raw_t.T + dtb_ref[...]
    dt = jnp.maximum(u, 0.0) + jnp.log1p(jnp.exp(-jnp.abs(u)))
    a = dt * (-jnp.exp(alog_ref[...]) * LOG2E)
    a1, a2, a3 = _split3(a)
    a_parts = jnp.concatenate([a1, a2, a3], axis=0).astype(BF16)
    a_cum = jnp.dot(tril3_ref[...], a_parts, preferred_element_type=F32)
    return dt, a_cum


def _decay_weights(dt, a_cum, e3_ref):
    L = SSD_L
    a_last = a_cum[L - 1:L, :]
    w = dt * jnp.exp2(a_last - a_cum)
    w_exp = _expand_heads(w, e3_ref)
    cd = _expand_heads(jnp.broadcast_to(jnp.exp2(a_last), (SUBLANES, LANES)), e3_ref)[0:1, :]
    return w_exp, cd


def _ssd_front(tiles, dt, a_cum, w_exp, cd):
    xc = jnp.concatenate(tiles, axis=1)
    xs = xc[:, :SSD_WIDTH]
    xs_bf = xs.astype(BF16)
    bm = xc[:, SSD_WIDTH:SSD_WIDTH + SSD_GROUPS * SSD_STATE]
    cm = xc[:, SSD_WIDTH + SSD_GROUPS * SSD_STATE:]
    xd = (xs * w_exp).astype(BF16)
    return xs, xs_bf, bm, cm, a_cum, a_cum.T, dt.T, jnp.exp2(a_cum), xd, cd


def _ssd_back(c, front, gz_ref, dexp_ref, y_ref, state, causal, low, hooks, tick):
    L = SSD_L
    gw = SSD_WIDTH // SSD_GROUPS
    ppg = SSD_HEADS // SSD_GROUPS // 2
    rows = slice(c * L, (c + 1) * L)
    xs, xs_bf, bm, cm, a_cum, a_cum_t, dt_t, exp_acum, xd, cd = front

    def finish(p, lhs, rhs):
        yp = jnp.dot(lhs, rhs, preferred_element_type=F32)
        yp = yp + dexp_ref[:, p * PAIR:(p + 1) * PAIR] * xs[:, p * PAIR:(p + 1) * PAIR]
        gz = gz_ref[0, rows, p * PAIR:(p + 1) * PAIR].astype(F32)
        y_ref[0, rows, p * PAIR:(p + 1) * PAIR] = (yp * gz).astype(BF16)

    zero = jnp.zeros((L, PAIR), BF16)
    pending = None
    for g in range(SSD_GROUPS):
        bg = bm[:, g * SSD_STATE:(g + 1) * SSD_STATE]
        cg = cm[:, g * SSD_STATE:(g + 1) * SSD_STATE]
        cb = lax.dot_general(cg.astype(BF16), bg.astype(BF16), (((1,), (1,)), ((), ())),
                             preferred_element_type=F32)
        sg = state[g]
        sg_bf = sg.astype(BF16)
        for q in range(ppg):
            p = g * ppg + q
            lhs = []
            for h in (2 * p, 2 * p + 1):
                seg = a_cum[:, h:h + 1] - a_cum_t[h:h + 1, :]
                dec = jnp.where(causal, jnp.exp2(seg), 0.0)
                lhs.append((cb * dec * dt_t[h:h + 1, :]).astype(BF16))
            for h in (2 * p, 2 * p + 1):
                lhs.append((cg * exp_acum[:, h:h + 1]).astype(BF16))
            xp = xs_bf[:, p * PAIR:(p + 1) * PAIR]
            sp = sg_bf[:, q * PAIR:(q + 1) * PAIR]
            rhs = jnp.concatenate([jnp.where(low, xp, zero), jnp.where(low, zero, xp),
                                   jnp.where(low, sp, zero), jnp.where(low, zero, sp)], axis=0)
            lhs = jnp.concatenate(lhs, axis=1)
            if pending is not None:
                finish(*pending)
            pending = (p, lhs, rhs)
            if p < len(hooks):
                hooks[p]()
            tick()
        finish(*pending)
        pending = None
        s_new = jnp.dot(bg.T.astype(BF16), xd[:, g * gw:(g + 1) * gw],
                        preferred_element_type=F32)
        state[g] = sg * cd[:, g * gw:(g + 1) * gw] + s_new


def _ssd_steps(j, gz_ref, xbc_ref, dtt_ref, cw_ref, cb_ref, dtb_ref, alog_ref,
               dexp_ref, e3_ref, tril3_ref, shift_ref, y_ref, ubuf, state, tick):
    T = SSD_T
    L = SSD_L

    @pl.when(j == 0)
    def _():
        ubuf[0:L, :] = jnp.zeros((L, CONV_DIM), BF16)
        state[...] = jnp.zeros(state.shape, F32)

    @pl.when(j > 0)
    def _():
        ubuf[0:L, :] = ubuf[T:T + L, :]

    ubuf[L:T + L, :] = xbc_ref[0]

    row = lax.broadcasted_iota(jnp.int32, (L, L), 0)
    col = lax.broadcasted_iota(jnp.int32, (L, L), 1)
    causal = row >= col
    low = lax.broadcasted_iota(jnp.int32, (L, PAIR), 1) < SSD_HEAD_DIM
    def conv(c, k):
        return _conv_tile(c, k, cw_ref, cb_ref, shift_ref, ubuf)

    def cumsum(c):
        return _decay_cumsum(c, dtt_ref, dtb_ref, alog_ref, tril3_ref)

    n = T // L
    dt, a_cum = cumsum(0)
    tiles = [conv(0, k) for k in range(N_CONV_TILES)]
    cur = _ssd_front(tiles, dt, a_cum, *_decay_weights(dt, a_cum, e3_ref))
    for c in range(n):
        nxt = {}
        hooks = []
        if c + 1 < n:
            def h_cumsum(cc=c + 1):
                nxt["dt"], nxt["a_cum"] = cumsum(cc)

            def h_conv(k, cc=c + 1):
                nxt.setdefault("tiles", []).append(conv(cc, k))

            def h_weights():
                nxt["w"] = _decay_weights(nxt["dt"], nxt["a_cum"], e3_ref)

            hooks = ([h_cumsum] + [functools.partial(h_conv, k) for k in range(N_CONV_TILES)]
                     + [h_weights])
        _ssd_back(c, cur, gz_ref, dexp_ref, y_ref, state, causal, low, hooks, tick)
        if c + 1 < n:
            cur = _ssd_front(nxt["tiles"], nxt["dt"], nxt["a_cum"], *nxt["w"])


def _mix_kernel(q_ref, kt_ref, v_ref, tab_ref, gz_ref, xbc_ref, dtt_ref, cw_ref, cb_ref,
                dtb_ref, alog_ref, dexp_ref, e3_ref, tril3_ref, shift_ref,
                att_ref, y_ref, kring, vring, ubuf, state):
    i = pl.program_id(1)
    att = _attn_steps(i, q_ref, kt_ref, v_ref, tab_ref, att_ref, kring, vring)
    pairs_done = [0]

    def tick():
        pairs_done[0] += 1
        if pairs_done[0] % MIX_PAIRS_PER_ITEM == 0:
            next(att, None)

    next(att)
    _ssd_steps(i, gz_ref, xbc_ref, dtt_ref, cw_ref, cb_ref, dtb_ref, alog_ref, dexp_ref,
               e3_ref, tril3_ref, shift_ref, y_ref, ubuf, state, tick)
    for _ in att:
        pass


def _mixers(q, kt, v, table, gz, xbc, dtt_raw, conv_w, conv_b, dt_bias_p, a_log_p, d_exp,
            e3, tril3, shift):
    b, s, _ = gz.shape
    T = SSD_T
    assert T == ATT_NB * ATT_TQ
    nblk = s // T
    blk = lambda bi, j: (bi, j, 0)
    colblk = lambda bi, j: (0, bi * nblk + j)
    const = lambda bi, j: (0, 0)
    once = pl.Buffered(1)
    return pl.pallas_call(
        _mix_kernel,
        out_shape=(jax.ShapeDtypeStruct((b, s, ATT_WIDTH), BF16),
                   jax.ShapeDtypeStruct((b, s, SSD_WIDTH), BF16)),
        grid=(b, nblk),
        in_specs=[
            pl.BlockSpec((1, T, ATT_WIDTH), blk),
            pl.BlockSpec((ATT_WIDTH, T), colblk),
            pl.BlockSpec((1, T, ATT_WIDTH), blk),
            pl.BlockSpec((ATT_HEADS + 1, ATT_TQ, ATT_BAND), lambda bi, j: (0, 0, 0),
                         pipeline_mode=once),
            pl.BlockSpec((1, T, SSD_WIDTH), blk),
            pl.BlockSpec((1, T, CONV_DIM), blk),
            pl.BlockSpec((SSD_HEADS, T), colblk),
            pl.BlockSpec((SSD_CONV, CONV_DIM), const),
            pl.BlockSpec((1, CONV_DIM), const),
            pl.BlockSpec((1, LANES), const),
            pl.BlockSpec((1, LANES), const),
            pl.BlockSpec((1, SSD_WIDTH), const),
            pl.BlockSpec((LANES, SSD_WIDTH), const, pipeline_mode=once),
            pl.BlockSpec((SSD_L, 3 * SSD_L), const, pipeline_mode=once),
            pl.BlockSpec(((SSD_CONV - 1) * SSD_L, 2 * SSD_L), const, pipeline_mode=once),
        ],
        out_specs=(pl.BlockSpec((1, T, ATT_WIDTH), blk),
                   pl.BlockSpec((1, T, SSD_WIDTH), blk)),
        scratch_shapes=[
            pltpu.VMEM((ATT_RING, ATT_HEADS * PAIR, ATT_TQ), BF16),
            pltpu.VMEM((ATT_RING, ATT_TQ, ATT_HEADS * PAIR), BF16),
            pltpu.VMEM((T + SSD_L, CONV_DIM), BF16),
            pltpu.VMEM((SSD_GROUPS, SSD_STATE, SSD_WIDTH // SSD_GROUPS), F32),
        ],
        compiler_params=pltpu.CompilerParams(
            dimension_semantics=("parallel", "arbitrary"),
            vmem_limit_bytes=VMEM_LIMIT),
        name="mixers",
    )(q, kt, v, table, gz, xbc, dtt_raw, conv_w, conv_b, dt_bias_p, a_log_p, d_exp,
      e3, tril3, shift)


def _shift_matrix():
    r = jnp.arange((SSD_CONV - 1) * SSD_L)[:, None]
    c = jnp.arange(2 * SSD_L)[None, :]
    return (c == SSD_L + r % SSD_L - (r // SSD_L + 1)).astype(BF16)


def _tril3_matrix():
    r = jnp.arange(SSD_L)[:, None]
    c = jnp.arange(3 * SSD_L)[None, :] % SSD_L
    return (r >= c).astype(BF16)


def _expand_matrix():
    r = jnp.arange(LANES)[:, None]
    c = jnp.arange(SSD_WIDTH)[None, :]
    hit = (r < 3 * SSD_HEADS) & ((r % SSD_HEADS) == c // SSD_HEAD_DIM)
    return hit.astype(BF16)


def _out_ffn_kernel(x_ref, att_ref, ssd_ref, sw_ref, wo_ref, nw_ref,
                    w1_ref, w2_ref, fw_ref, o_ref):
    gw = SSD_WIDTH // SSD_GROUPS
    ssd = []
    for g in range(SSD_GROUPS):
        yg = ssd_ref[:, g * gw:(g + 1) * gw].astype(F32)
        ssd.append(_rms(yg, sw_ref[:, g * gw:(g + 1) * gw]).astype(BF16))
    ssd = jnp.concatenate(ssd, axis=1)
    h = (x_ref[...]
         + jnp.dot(att_ref[...], wo_ref[:ATT_WIDTH, :], preferred_element_type=F32)
         + jnp.dot(ssd, wo_ref[ATT_WIDTH:, :], preferred_element_type=F32))
    hn = _rms(h, nw_ref[...]).astype(BF16)
    hid = jnp.dot(hn, w1_ref[...], preferred_element_type=F32)
    hid = jnp.square(jnp.maximum(hid, 0.0)).astype(BF16)
    h = h + jnp.dot(hid, w2_ref[...], preferred_element_type=F32)
    o_ref[...] = _rms(h, fw_ref[...])


def _out_ffn(x2, att2, ssd2, ssd_norm_w, w_out, norm_w, w1, w2, final_w):
    m = x2.shape[0]
    tm = FFN_TM
    row = lambda i: (i, 0)
    const = lambda i: (0, 0)
    once = pl.Buffered(1)
    return pl.pallas_call(
        _out_ffn_kernel,
        out_shape=jax.ShapeDtypeStruct((m, D_MODEL), F32),
        grid=(m // tm,),
        in_specs=[
            pl.BlockSpec((tm, D_MODEL), row),
            pl.BlockSpec((tm, ATT_WIDTH), row),
            pl.BlockSpec((tm, SSD_WIDTH), row),
            pl.BlockSpec((1, SSD_WIDTH), const),
            pl.BlockSpec((ATT_WIDTH + SSD_WIDTH, D_MODEL), const, pipeline_mode=once),
            pl.BlockSpec((1, D_MODEL), const),
            pl.BlockSpec((D_MODEL, D_FF), const, pipeline_mode=once),
            pl.BlockSpec((D_FF, D_MODEL), const, pipeline_mode=once),
            pl.BlockSpec((1, D_MODEL), const),
        ],
        out_specs=pl.BlockSpec((tm, D_MODEL), row),
        compiler_params=pltpu.CompilerParams(
            dimension_semantics=("parallel",), vmem_limit_bytes=VMEM_LIMIT),
        name="out_ffn",
    )(x2, att2, ssd2, ssd_norm_w, w_out, norm_w, w1, w2, final_w)


def _layer(x, norm_mix_w, w_in, rel_bias, conv_w, conv_b, dt_bias, a_log, d_skip,
           ssd_norm_w, w_out, norm_mlp_w, w_ff1, w_ff2, out_norm_w):
    b, s, d = x.shape
    m = b * s
    x2 = x.reshape(m, d)

    scale = ATT_HEAD_DIM ** -0.5 * LOG2E
    col = jnp.arange(MAIN_PROJ + SSD_HEADS)[None, :]
    w_main = (w_in * jnp.where(col < ATT_WIDTH, scale, 1.0)).astype(BF16)
    w_ktd = jnp.concatenate([w_in[:, ATT_WIDTH:2 * ATT_WIDTH], w_in[:, MAIN_PROJ:]],
                            axis=1).T.astype(BF16)
    pad_h = lambda v: jnp.pad(v.astype(F32), (0, LANES - SSD_HEADS)).reshape(1, LANES)

    q, kt, v, gz, xbc, dtt_raw = _in_proj(x2, norm_mix_w.reshape(1, d), w_main, w_ktd)

    att, ssd = _mixers(q.reshape(b, s, -1), kt, v.reshape(b, s, -1), _bias_table(rel_bias),
                       gz.reshape(b, s, -1), xbc.reshape(b, s, -1), dtt_raw,
                       conv_w.astype(F32), conv_b.reshape(1, -1).astype(F32),
                       pad_h(dt_bias), pad_h(a_log),
                       jnp.repeat(d_skip.astype(F32), SSD_HEAD_DIM).reshape(1, -1),
                       _expand_matrix(), _tril3_matrix(), _shift_matrix())

    y = _out_ffn(x2, att.reshape(m, -1), ssd.reshape(m, -1),
                 ssd_norm_w.reshape(1, -1).astype(F32),
                 w_out.astype(BF16),
                 norm_mlp_w.reshape(1, d), w_ff1.astype(BF16), w_ff2.astype(BF16),
                 out_norm_w.reshape(1, d))
    return y.reshape(b, s, d)


def kernel(x, norm_mix_w, w_in, rel_bias, conv_w, conv_b, dt_bias, a_log, d_skip,
           ssd_norm_w, w_out, norm_mlp_w, w_ff1, w_ff2, norm_final_w):
    depth = w_in.shape[0]
    assert depth == 1, "final norm is fused into the single layer's last kernel"
    return _layer(x, norm_mix_w[0], w_in[0], rel_bias[0], conv_w[0], conv_b[0],
                  dt_bias[0], a_log[0], d_skip[0], ssd_norm_w[0], w_out[0],
                  norm_mlp_w[0], w_ff1[0], w_ff2[0], norm_final_w)
```

```python
import functools

import jax
import jax.numpy as jnp
from jax import lax
from jax.experimental import pallas as pl
from jax.experimental.pallas import tpu as pltpu

F32 = jnp.float32
BF16 = jnp.bfloat16

D_MODEL = 1024
CHUNK = 64
ATT_HEADS = 8
ATT_HEAD_DIM = 64
ATT_WIDTH = ATT_HEADS * ATT_HEAD_DIM
N_PREV_CHUNKS = 8
REL_CLIP = 256
SSD_HEAD_DIM = 64
SSD_WIDTH = D_MODEL
SSD_HEADS = SSD_WIDTH // SSD_HEAD_DIM
SSD_GROUPS = 2
SSD_STATE = 128
SSD_CONV = 4
CONV_DIM = SSD_WIDTH + 2 * SSD_GROUPS * SSD_STATE
MAIN_PROJ = 3 * ATT_WIDTH + SSD_WIDTH + CONV_DIM
D_FF = 4 * D_MODEL
EPS = 1e-5
NEG = -1e30

LANES = 128
SUBLANES = 8

PROJ_TM = 1024
ATT_TQ = 256
ATT_BAND = ATT_TQ + N_PREV_CHUNKS * CHUNK
SSD_L = 128
SSD_T = 1024
MIX_PAIRS_PER_ITEM = 1
FFN_TM = 1024
VMEM_LIMIT = 56 * 1024 * 1024


def _rms(x, w):
    ms = jnp.mean(x * x, axis=-1, keepdims=True)
    return x * lax.rsqrt(ms + EPS) * w


def _silu(x):
    h = 0.5 * x
    return h + h * jnp.tanh(h)


def _in_proj_kernel(x_ref, nw_ref, wm_ref, wkt_ref,
                    q_ref, kt_ref, v_ref, gz_ref, xbc_ref, dtt_ref):
    hn = _rms(x_ref[...], nw_ref[...]).astype(BF16)

    def seg(lo, hi):
        return jnp.dot(hn, wm_ref[:, lo:hi], preferred_element_type=F32).astype(BF16)

    a = ATT_WIDTH
    zlo = 3 * a
    xlo = zlo + SSD_WIDTH
    q_ref[...] = seg(0, a)
    ktd = lax.dot_general(wkt_ref[...], hn, (((1,), (1,)), ((), ())),
                          preferred_element_type=F32)
    kt_ref[...] = ktd[:a, :].astype(BF16)
    dtt_ref[...] = ktd[a:, :]
    v_ref[...] = seg(2 * a, 3 * a)
    gz_ref[...] = _silu(jnp.dot(hn, wm_ref[:, zlo:xlo],
                                preferred_element_type=F32)).astype(BF16)
    xbc_ref[...] = seg(xlo, MAIN_PROJ)


def _in_proj(x2, norm_w, w_main, w_ktd):
    m = x2.shape[0]
    tm = PROJ_TM
    row = lambda i: (i, 0)
    col = lambda i: (0, i)
    const = lambda i: (0, 0)
    once = pl.Buffered(1)
    out_shape = (
        jax.ShapeDtypeStruct((m, ATT_WIDTH), BF16),
        jax.ShapeDtypeStruct((ATT_WIDTH, m), BF16),
        jax.ShapeDtypeStruct((m, ATT_WIDTH), BF16),
        jax.ShapeDtypeStruct((m, SSD_WIDTH), BF16),
        jax.ShapeDtypeStruct((m, CONV_DIM), BF16),
        jax.ShapeDtypeStruct((SSD_HEADS, m), F32),
    )
    return pl.pallas_call(
        _in_proj_kernel,
        out_shape=out_shape,
        grid=(m // tm,),
        in_specs=[
            pl.BlockSpec((tm, D_MODEL), row),
            pl.BlockSpec((1, D_MODEL), const),
            pl.BlockSpec((D_MODEL, MAIN_PROJ + SSD_HEADS), const, pipeline_mode=once),
            pl.BlockSpec((ATT_WIDTH + SSD_HEADS, D_MODEL), const, pipeline_mode=once),
        ],
        out_specs=(
            pl.BlockSpec((tm, ATT_WIDTH), row),
            pl.BlockSpec((ATT_WIDTH, tm), col),
            pl.BlockSpec((tm, ATT_WIDTH), row),
            pl.BlockSpec((tm, SSD_WIDTH), row),
            pl.BlockSpec((tm, CONV_DIM), row),
            pl.BlockSpec((SSD_HEADS, tm), col),
        ),
        compiler_params=pltpu.CompilerParams(
            dimension_semantics=("parallel",), vmem_limit_bytes=VMEM_LIMIT),
        name="in_proj",
    )(x2, norm_w, w_main, w_ktd)


ATT_SLOTS = ATT_BAND // ATT_TQ
ATT_NB = 4
ATT_RING = ATT_NB + ATT_SLOTS - 1
ATT_AHEAD = 2
PAIR = 2 * ATT_HEAD_DIM
LOG2E = 1.4426950408889634


def _bias_table_kernel(r_ref, o_ref):
    x = jnp.broadcast_to(r_ref[0], (ATT_TQ, ATT_BAND + ATT_TQ))
    rolled = pltpu.roll(x, 0, 1, stride=1, stride_axis=0)
    t = rolled[:, ATT_TQ:]
    qc = lax.broadcasted_iota(jnp.int32, (ATT_TQ, ATT_BAND), 0) // CHUNK
    kc = lax.broadcasted_iota(jnp.int32, (ATT_TQ, ATT_BAND), 1) // CHUNK
    valid = (kc >= qc) & (kc <= qc + N_PREV_CHUNKS)
    valid = valid & (pl.program_id(0) < ATT_HEADS)
    o_ref[0] = jnp.where(valid, t * LOG2E, NEG)


def _bias_table(rel_bias):
    rb = rel_bias.astype(F32).T
    n = ATT_BAND + ATT_TQ
    far = jnp.broadcast_to(rb[:, 2 * REL_CLIP:], (ATT_HEADS, n - 2 * REL_CLIP + 1))
    near = rb[:, 2 * REL_CLIP - 1:0:-1]
    r = jnp.concatenate([far, near], axis=1)
    r = jnp.concatenate([r, jnp.zeros((1, n), F32)], axis=0).reshape(ATT_HEADS + 1, 1, n)
    return pl.pallas_call(
        _bias_table_kernel,
        out_shape=jax.ShapeDtypeStruct((ATT_HEADS + 1, ATT_TQ, ATT_BAND), F32),
        grid=(ATT_HEADS + 1,),
        in_specs=[pl.BlockSpec((1, 1, n), lambda h: (h, 0, 0))],
        out_specs=pl.BlockSpec((1, ATT_TQ, ATT_BAND), lambda h: (h, 0, 0)),
        compiler_params=pltpu.CompilerParams(dimension_semantics=("parallel",)),
        name="bias_table",
    )(r)


def _attn_steps(i, q_ref, kt_ref, v_ref, tab_ref, o_ref, kring, vring):
    tq = ATT_TQ
    hd = ATT_HEAD_DIM
    lane = lax.broadcasted_iota(jnp.int32, (tq, PAIR), 1)
    low = lane < hd

    @pl.when(i == 0)
    def _():
        kring[...] = jnp.zeros(kring.shape, BF16)
        vring[...] = jnp.zeros(vring.shape, BF16)

    for j in range(ATT_NB):
        slot_new = lax.rem(ATT_NB * i + j, ATT_RING)
        cols = slice(j * tq, (j + 1) * tq)
        for h in range(ATT_HEADS):
            r0 = h * PAIR + (h % 2) * hd
            kring[slot_new, r0:r0 + hd, :] = kt_ref[h * hd:(h + 1) * hd, cols]
        for p in range(ATT_HEADS // 2):
            vp = v_ref[0, cols, p * PAIR:(p + 1) * PAIR]
            one = jnp.ones_like(vp)
            vring[slot_new, :, (2 * p) * PAIR:(2 * p + 1) * PAIR] = jnp.where(low, vp, one)
            vring[slot_new, :, (2 * p + 1) * PAIR:(2 * p + 2) * PAIR] = jnp.where(low, one, vp)

    def band(j):
        n = [ATT_NB * i + j - (ATT_SLOTS - 1) + m for m in range(ATT_SLOTS)]
        slots = [lax.rem(nm + ATT_RING, ATT_RING) for nm in n]
        masked = [nm < 0 for nm in n]
        return slots, masked

    bands = [band(j) for j in range(ATT_NB)]

    def scores(j, h):
        slots, masked = bands[j]
        p = h // 2
        qp = q_ref[0, j * tq:(j + 1) * tq, p * PAIR:(p + 1) * PAIR]
        return [jnp.dot(qp, kring[slots[m], h * PAIR:(h + 1) * PAIR, :],
                        preferred_element_type=F32)
                + tab_ref[jnp.where(masked[m], ATT_HEADS, h), :, m * tq:(m + 1) * tq]
                for m in range(ATT_SLOTS)]

    items = [(j, h) for j in range(ATT_NB) for h in range(ATT_HEADS)]
    queue = [scores(*it) for it in items[:ATT_AHEAD]]
    res = []
    yield
    for idx, (j, h) in enumerate(items):
        s = queue.pop(0)
        if idx + ATT_AHEAD < len(items):
            queue.append(scores(*items[idx + ATT_AHEAD]))
        slots, _ = bands[j]
        mx = functools.reduce(jnp.maximum, s)
        mx = jnp.max(mx, axis=-1, keepdims=True)
        acc = None
        for m in range(ATT_SLOTS):
            e = jnp.exp2(s[m] - mx).astype(BF16)
            pv = jnp.dot(e, vring[slots[m], :, h * PAIR:(h + 1) * PAIR],
                         preferred_element_type=F32)
            acc = pv if acc is None else acc + pv
        res.append(acc)
        if h % 2 == 1:
            p = h // 2
            num = jnp.where(low, res[0], res[1])
            den = jnp.where(low, pltpu.roll(res[0], hd, 1), pltpu.roll(res[1], hd, 1))
            o_ref[0, j * tq:(j + 1) * tq, p * PAIR:(p + 1) * PAIR] = (num / den).astype(BF16)
            res = []
        yield


def _split3(v):
    v1 = v.astype(BF16).astype(F32)
    r1 = v - v1
    v2 = r1.astype(BF16).astype(F32)
    v3 = (r1 - v2).astype(BF16).astype(F32)
    return v1, v2, v3


def _expand_heads(v, e3_ref):
    lane = lax.broadcasted_iota(jnp.int32, v.shape, 1)
    v = jnp.where(lane < SSD_HEADS, v, 0.0)
    v1, v2, v3 = _split3(v)
    packed = v1 + pltpu.roll(v2, SSD_HEADS, 1) + pltpu.roll(v3, 2 * SSD_HEADS, 1)
    return jnp.dot(packed.astype(BF16), e3_ref[...], preferred_element_type=F32)


CONV_TILE = 256
N_CONV_TILES = CONV_DIM // CONV_TILE


def _conv_tile(c, k, cw_ref, cb_ref, shift_ref, ubuf):
    L = SSD_L
    ch = slice(k * CONV_TILE, (k + 1) * CONV_TILE)
    win = ubuf[c * L:c * L + 2 * L, ch]
    sh = jnp.dot(shift_ref[...], win, preferred_element_type=F32)
    acc = cb_ref[:, ch] + cw_ref[SSD_CONV - 1:SSD_CONV, ch] * win[L:, :].astype(F32)
    for d in range(1, SSD_CONV):
        acc = acc + cw_ref[SSD_CONV - 1 - d:SSD_CONV - d, ch] * sh[(d - 1) * L:d * L, :]
    return _silu(acc)


def _decay_cumsum(c, dtt_ref, dtb_ref, alog_ref, tril3_ref):
    L = SSD_L
    rows = slice(c * L, (c + 1) * L)
    raw_t = jnp.concatenate([dtt_ref[:, rows], jnp.zeros((LANES - SSD_HEADS, L), F32)], axis=0)
    u = raw_t.T + dtb_ref[...]
    dt = jnp.maximum(u, 0.0) + jnp.log1p(jnp.exp(-jnp.abs(u)))
    a = dt * (-jnp.exp(alog_ref[...]) * LOG2E)
    a1, a2, a3 = _split3(a)
    a_parts = jnp.concatenate([a1, a2, a3], axis=0).astype(BF16)
    a_cum = jnp.dot(tril3_ref[...], a_parts, preferred_element_type=F32)
    return dt, a_cum


def _decay_weights(dt, a_cum, e3_ref):
    L = SSD_L
    a_last = a_cum[L - 1:L, :]
    w = dt * jnp.exp2(a_last - a_cum)
    w_exp = _expand_heads(w, e3_ref)
    cd = _expand_heads(jnp.broadcast_to(jnp.exp2(a_last), (SUBLANES, LANES)), e3_ref)[0:1, :]
    return w_exp, cd


def _ssd_front(tiles, dt, a_cum, w_exp, cd):
    xc = jnp.concatenate(tiles, axis=1)
    xs = xc[:, :SSD_WIDTH]
    xs_bf = xs.astype(BF16)
    bm = xc[:, SSD_WIDTH:SSD_WIDTH + SSD_GROUPS * SSD_STATE]
    cm = xc[:, SSD_WIDTH + SSD_GROUPS * SSD_STATE:]
    xd = (xs * w_exp).astype(BF16)
    return xs, xs_bf, bm, cm, a_cum, a_cum.T, dt.T, jnp.exp2(a_cum), xd, cd


def _ssd_back(c, front, gz_ref, dexp_ref, y_ref, state, causal, low, hooks, tick):
    L = SSD_L
    gw = SSD_WIDTH // SSD_GROUPS
    ppg = SSD_HEADS // SSD_GROUPS // 2
    rows = slice(c * L, (c + 1) * L)
    xs, xs_bf, bm, cm, a_cum, a_cum_t, dt_t, exp_acum, xd, cd = front

    def finish(p, lhs, rhs):
        yp = jnp.dot(lhs, rhs, preferred_element_type=F32)
        yp = yp + dexp_ref[:, p * PAIR:(p + 1) * PAIR] * xs[:, p * PAIR:(p + 1) * PAIR]
        gz = gz_ref[0, rows, p * PAIR:(p + 1) * PAIR].astype(F32)
        y_ref[0, rows, p * PAIR:(p + 1) * PAIR] = (yp * gz).astype(BF16)

    zero = jnp.zeros((L, PAIR), BF16)
    pending = None
    for g in range(SSD_GROUPS):
        bg = bm[:, g * SSD_STATE:(g + 1) * SSD_STATE]
        cg = cm[:, g * SSD_STATE:(g + 1) * SSD_STATE]
        cb = lax.dot_general(cg.astype(BF16), bg.astype(BF16), (((1,), (1,)), ((), ())),
                             preferred_element_type=F32)
        sg = state[g]
        sg_bf = sg.astype(BF16)
        for q in range(ppg):
            p = g * ppg + q
            lhs = []
            for h in (2 * p, 2 * p + 1):
                seg = a_cum[:, h:h + 1] - a_cum_t[h:h + 1, :]
                dec = jnp.where(causal, jnp.exp2(seg), 0.0)
                lhs.append((cb * dec * dt_t[h:h + 1, :]).astype(BF16))
            for h in (2 * p, 2 * p + 1):
                lhs.append((cg * exp_acum[:, h:h + 1]).astype(BF16))
            xp = xs_bf[:, p * PAIR:(p + 1) * PAIR]
            sp = sg_bf[:, q * PAIR:(q + 1) * PAIR]
            rhs = jnp.concatenate([jnp.where(low, xp, zero), jnp.where(low, zero, xp),
                                   jnp.where(low, sp, zero), jnp.where(low, zero, sp)], axis=0)
            lhs = jnp.concatenate(lhs, axis=1)
            if pending is not None:
                finish(*pending)
            pending = (p, lhs, rhs)
            if p < len(hooks):
                hooks[p]()
            tick()
        finish(*pending)
        pending = None
        s_new = jnp.dot(bg.T.astype(BF16), xd[:, g * gw:(g + 1) * gw],
                        preferred_element_type=F32)
        state[g] = sg * cd[:, g * gw:(g + 1) * gw] + s_new


def _ssd_steps(j, gz_ref, xbc_ref, dtt_ref, cw_ref, cb_ref, dtb_ref, alog_ref,
               dexp_ref, e3_ref, tril3_ref, shift_ref, y_ref, ubuf, state, tick):
    T = SSD_T
    L = SSD_L

    @pl.when(j == 0)
    def _():
        ubuf[0:L, :] = jnp.zeros((L, CONV_DIM), BF16)
        state[...] = jnp.zeros(state.shape, F32)

    @pl.when(j > 0)
    def _():
        ubuf[0:L, :] = ubuf[T:T + L, :]

    ubuf[L:T + L, :] = xbc_ref[0]

    row = lax.broadcasted_iota(jnp.int32, (L, L), 0)
    col = lax.broadcasted_iota(jnp.int32, (L, L), 1)
    causal = row >= col
    low = lax.broadcasted_iota(jnp.int32, (L, PAIR), 1) < SSD_HEAD_DIM
    def conv(c, k):
        return _conv_tile(c, k, cw_ref, cb_ref, shift_ref, ubuf)

    def cumsum(c):
        return _decay_cumsum(c, dtt_ref, dtb_ref, alog_ref, tril3_ref)

    n = T // L
    dt, a_cum = cumsum(0)
    tiles = [conv(0, k) for k in range(N_CONV_TILES)]
    cur = _ssd_front(tiles, dt, a_cum, *_decay_weights(dt, a_cum, e3_ref))
    for c in range(n):
        nxt = {}
        hooks = []
        if c + 1 < n:
            def h_cumsum(cc=c + 1):
                nxt["dt"], nxt["a_cum"] = cumsum(cc)

            def h_conv(k, cc=c + 1):
                nxt.setdefault("tiles", []).append(conv(cc, k))

            def h_weights():
                nxt["w"] = _decay_weights(nxt["dt"], nxt["a_cum"], e3_ref)

            hooks = ([h_cumsum] + [functools.partial(h_conv, k) for k in range(N_CONV_TILES)]
                     + [h_weights])
        _ssd_back(c, cur, gz_ref, dexp_ref, y_ref, state, causal, low, hooks, tick)
        if c + 1 < n:
            cur = _ssd_front(nxt["tiles"], nxt["dt"], nxt["a_cum"], *nxt["w"])


def _mix_kernel(q_ref, kt_ref, v_ref, tab_ref, gz_ref, xbc_ref, dtt_ref, cw_ref, cb_ref,
                dtb_ref, alog_ref, dexp_ref, e3_ref, tril3_ref, shift_ref,
                att_ref, y_ref, kring, vring, ubuf, state):
    i = pl.program_id(1)
    att = _attn_steps(i, q_ref, kt_ref, v_ref, tab_ref, att_ref, kring, vring)
    pairs_done = [0]

    def tick():
        pairs_done[0] += 1
        if pairs_done[0] % MIX_PAIRS_PER_ITEM == 0:
            next(att, None)

    next(att)
    _ssd_steps(i, gz_ref, xbc_ref, dtt_ref, cw_ref, cb_ref, dtb_ref, alog_ref, dexp_ref,
               e3_ref, tril3_ref, shift_ref, y_ref, ubuf, state, tick)
    for _ in att:
        pass


def _mixers(q, kt, v, table, gz, xbc, dtt_raw, conv_w, conv_b, dt_bias_p, a_log_p, d_exp,
            e3, tril3, shift):
    b, s, _ = gz.shape
    T = SSD_T
    assert T == ATT_NB * ATT_TQ
    nblk = s // T
    blk = lambda bi, j: (bi, j, 0)
    colblk = lambda bi, j: (0, bi * nblk + j)
    const = lambda bi, j: (0, 0)
    once = pl.Buffered(1)
    return pl.pallas_call(
        _mix_kernel,
        out_shape=(jax.ShapeDtypeStruct((b, s, ATT_WIDTH), BF16),
                   jax.ShapeDtypeStruct((b, s, SSD_WIDTH), BF16)),
        grid=(b, nblk),
        in_specs=[
            pl.BlockSpec((1, T, ATT_WIDTH), blk),
            pl.BlockSpec((ATT_WIDTH, T), colblk),
            pl.BlockSpec((1, T, ATT_WIDTH), blk),
            pl.BlockSpec((ATT_HEADS + 1, ATT_TQ, ATT_BAND), lambda bi, j: (0, 0, 0),
                         pipeline_mode=once),
            pl.BlockSpec((1, T, SSD_WIDTH), blk),
            pl.BlockSpec((1, T, CONV_DIM), blk),
            pl.BlockSpec((SSD_HEADS, T), colblk),
            pl.BlockSpec((SSD_CONV, CONV_DIM), const),
            pl.BlockSpec((1, CONV_DIM), const),
            pl.BlockSpec((1, LANES), const),
            pl.BlockSpec((1, LANES), const),
            pl.BlockSpec((1, SSD_WIDTH), const),
            pl.BlockSpec((LANES, SSD_WIDTH), const, pipeline_mode=once),
            pl.BlockSpec((SSD_L, 3 * SSD_L), const, pipeline_mode=once),
            pl.BlockSpec(((SSD_CONV - 1) * SSD_L, 2 * SSD_L), const, pipeline_mode=once),
        ],
        out_specs=(pl.BlockSpec((1, T, ATT_WIDTH), blk),
                   pl.BlockSpec((1, T, SSD_WIDTH), blk)),
        scratch_shapes=[
            pltpu.VMEM((ATT_RING, ATT_HEADS * PAIR, ATT_TQ), BF16),
            pltpu.VMEM((ATT_RING, ATT_TQ, ATT_HEADS * PAIR), BF16),
            pltpu.VMEM((T + SSD_L, CONV_DIM), BF16),
            pltpu.VMEM((SSD_GROUPS, SSD_STATE, SSD_WIDTH // SSD_GROUPS), F32),
        ],
        compiler_params=pltpu.CompilerParams(
            dimension_semantics=("parallel", "arbitrary"),
            vmem_limit_bytes=VMEM_LIMIT),
        name="mixers",
    )(q, kt, v, table, gz, xbc, dtt_raw, conv_w, conv_b, dt_bias_p, a_log_p, d_exp,
      e3, tril3, shift)


def _shift_matrix():
    r = jnp.arange((SSD_CONV - 1) * SSD_L)[:, None]
    c = jnp.arange(2 * SSD_L)[None, :]
    return (c == SSD_L + r % SSD_L - (r // SSD_L + 1)).astype(BF16)


def _tril3_matrix():
    r = jnp.arange(SSD_L)[:, None]
    c = jnp.arange(3 * SSD_L)[None, :] % SSD_L
    return (r >= c).astype(BF16)


def _expand_matrix():
    r = jnp.arange(LANES)[:, None]
    c = jnp.arange(SSD_WIDTH)[None, :]
    hit = (r < 3 * SSD_HEADS) & ((r % SSD_HEADS) == c // SSD_HEAD_DIM)
    return hit.astype(BF16)


def _out_ffn_kernel(x_ref, att_ref, ssd_ref, sw_ref, wo_ref, nw_ref,
                    w1_ref, w2_ref, fw_ref, o_ref):
    gw = SSD_WIDTH // SSD_GROUPS
    ssd = []
    for g in range(SSD_GROUPS):
        yg = ssd_ref[:, g * gw:(g + 1) * gw].astype(F32)
        ssd.append(_rms(yg, sw_ref[:, g * gw:(g + 1) * gw]).astype(BF16))
    ssd = jnp.concatenate(ssd, axis=1)
    h = (x_ref[...]
         + jnp.dot(att_ref[...], wo_ref[:ATT_WIDTH, :], preferred_element_type=F32)
         + jnp.dot(ssd, wo_ref[ATT_WIDTH:, :], preferred_element_type=F32))
    hn = _rms(h, nw_ref[...]).astype(BF16)
    hid = jnp.dot(hn, w1_ref[...], preferred_element_type=F32)
    hid = jnp.square(jnp.maximum(hid, 0.0)).astype(BF16)
    h = h + jnp.dot(hid, w2_ref[...], preferred_element_type=F32)
    o_ref[...] = _rms(h, fw_ref[...])


def _out_ffn(x2, att2, ssd2, ssd_norm_w, w_out, norm_w, w1, w2, final_w):
    m = x2.shape[0]
    tm = FFN_TM
    row = lambda i: (i, 0)
    const = lambda i: (0, 0)
    once = pl.Buffered(1)
    return pl.pallas_call(
        _out_ffn_kernel,
        out_shape=jax.ShapeDtypeStruct((m, D_MODEL), F32),
        grid=(m // tm,),
        in_specs=[
            pl.BlockSpec((tm, D_MODEL), row),
            pl.BlockSpec((tm, ATT_WIDTH), row),
            pl.BlockSpec((tm, SSD_WIDTH), row),
            pl.BlockSpec((1, SSD_WIDTH), const),
            pl.BlockSpec((ATT_WIDTH + SSD_WIDTH, D_MODEL), const, pipeline_mode=once),
            pl.BlockSpec((1, D_MODEL), const),
            pl.BlockSpec((D_MODEL, D_FF), const, pipeline_mode=once),
            pl.BlockSpec((D_FF, D_MODEL), const, pipeline_mode=once),
            pl.BlockSpec((1, D_MODEL), const),
        ],
        out_specs=pl.BlockSpec((tm, D_MODEL), row),
        compiler_params=pltpu.CompilerParams(
            dimension_semantics=("parallel",), vmem_limit_bytes=VMEM_LIMIT),
        name="out_ffn",
    )(x2, att2, ssd2, ssd_norm_w, w_out, norm_w, w1, w2, final_w)


def _layer(x, norm_mix_w, w_in, rel_bias, conv_w, conv_b, dt_bias, a_log, d_skip,
           ssd_norm_w, w_out, norm_mlp_w, w_ff1, w_ff2, out_norm_w):
    b, s, d = x.shape
    m = b * s
    x2 = x.reshape(m, d)

    scale = ATT_HEAD_DIM ** -0.5 * LOG2E
    col = jnp.arange(MAIN_PROJ + SSD_HEADS)[None, :]
    w_main = (w_in * jnp.where(col < ATT_WIDTH, scale, 1.0)).astype(BF16)
    w_ktd = jnp.concatenate([w_in[:, ATT_WIDTH:2 * ATT_WIDTH], w_in[:, MAIN_PROJ:]],
                            axis=1).T.astype(BF16)
    pad_h = lambda v: jnp.pad(v.astype(F32), (0, LANES - SSD_HEADS)).reshape(1, LANES)

    q, kt, v, gz, xbc, dtt_raw = _in_proj(x2, norm_mix_w.reshape(1, d), w_main, w_ktd)

    att, ssd = _mixers(q.reshape(b, s, -1), kt, v.reshape(b, s, -1), _bias_table(rel_bias),
                       gz.reshape(b, s, -1), xbc.reshape(b, s, -1), dtt_raw,
                       conv_w.astype(F32), conv_b.reshape(1, -1).astype(F32),
                       pad_h(dt_bias), pad_h(a_log),
                       jnp.repeat(d_skip.astype(F32), SSD_HEAD_DIM).reshape(1, -1),
                       _expand_matrix(), _tril3_matrix(), _shift_matrix())

    y = _out_ffn(x2, att.reshape(m, -1), ssd.reshape(m, -1),
                 ssd_norm_w.reshape(1, -1).astype(F32),
                 w_out.astype(BF16),
                 norm_mlp_w.reshape(1, d), w_ff1.astype(BF16), w_ff2.astype(BF16),
                 out_norm_w.reshape(1, d))
    return y.reshape(b, s, d)


def kernel(x, norm_mix_w, w_in, rel_bias, conv_w, conv_b, dt_bias, a_log, d_skip,
           ssd_norm_w, w_out, norm_mlp_w, w_ff1, w_ff2, norm_final_w):
    depth = w_in.shape[0]
    assert depth == 1, "final norm is fused into the single layer's last kernel"
    return _layer(x, norm_mix_w[0], w_in[0], rel_bias[0], conv_w[0], conv_b[0],
                  dt_bias[0], a_log[0], d_skip[0], ssd_norm_w[0], w_out[0],
                  norm_mlp_w[0], w_ff1[0], w_ff2[0], norm_final_w)
```

```python
import functools

import jax
import jax.numpy as jnp
from jax import lax
from jax.experimental import pallas as pl
from jax.experimental.pallas import tpu as pltpu

F32 = jnp.float32
BF16 = jnp.bfloat16

D_MODEL = 1024
CHUNK = 64
ATT_HEADS = 8
ATT_HEAD_DIM = 64
ATT_WIDTH = ATT_HEADS * ATT_HEAD_DIM
N_PREV_CHUNKS = 8
REL_CLIP = 256
SSD_HEAD_DIM = 64
SSD_WIDTH = D_MODEL
SSD_HEADS = SSD_WIDTH // SSD_HEAD_DIM
SSD_GROUPS = 2
SSD_STATE = 128
SSD_CONV = 4
CONV_DIM = SSD_WIDTH + 2 * SSD_GROUPS * SSD_STATE
MAIN_PROJ = 3 * ATT_WIDTH + SSD_WIDTH + CONV_DIM
D_FF = 4 * D_MODEL
EPS = 1e-5
NEG = -1e30

LANES = 128
SUBLANES = 8

PROJ_TM = 1024
ATT_TQ = 256
ATT_BAND = ATT_TQ + N_PREV_CHUNKS * CHUNK
SSD_L = 128
SSD_T = 1024
FFN_TM = 1024
VMEM_LIMIT = 56 * 1024 * 1024


def _rms(x, w):
    ms = jnp.mean(x * x, axis=-1, keepdims=True)
    return x * lax.rsqrt(ms + EPS) * w


def _silu(x):
    h = 0.5 * x
    return h + h * jnp.tanh(h)


def _in_proj_kernel(x_ref, nw_ref, wm_ref, wkt_ref,
                    q_ref, kt_ref, v_ref, gz_ref, xbc_ref, dtt_ref):
    hn = _rms(x_ref[...], nw_ref[...]).astype(BF16)

    def seg(lo, hi):
        return jnp.dot(hn, wm_ref[:, lo:hi], preferred_element_type=F32).astype(BF16)

    a = ATT_WIDTH
    zlo = 3 * a
    xlo = zlo + SSD_WIDTH
    q_ref[...] = seg(0, a)
    ktd = lax.dot_general(wkt_ref[...], hn, (((1,), (1,)), ((), ())),
                          preferred_element_type=F32)
    kt_ref[...] = ktd[:a, :].astype(BF16)
    dtt_ref[...] = ktd[a:, :]
    v_ref[...] = seg(2 * a, 3 * a)
    gz_ref[...] = _silu(jnp.dot(hn, wm_ref[:, zlo:xlo],
                                preferred_element_type=F32)).astype(BF16)
    xbc_ref[...] = seg(xlo, MAIN_PROJ)


def _in_proj(x2, norm_w, w_main, w_ktd):
    m = x2.shape[0]
    tm = PROJ_TM
    row = lambda i: (i, 0)
    col = lambda i: (0, i)
    const = lambda i: (0, 0)
    once = pl.Buffered(1)
    out_shape = (
        jax.ShapeDtypeStruct((m, ATT_WIDTH), BF16),
        jax.ShapeDtypeStruct((ATT_WIDTH, m), BF16),
        jax.ShapeDtypeStruct((m, ATT_WIDTH), BF16),
        jax.ShapeDtypeStruct((m, SSD_WIDTH), BF16),
        jax.ShapeDtypeStruct((m, CONV_DIM), BF16),
        jax.ShapeDtypeStruct((SSD_HEADS, m), F32),
    )
    return pl.pallas_call(
        _in_proj_kernel,
        out_shape=out_shape,
        grid=(m // tm,),
        in_specs=[
            pl.BlockSpec((tm, D_MODEL), row),
            pl.BlockSpec((1, D_MODEL), const),
            pl.BlockSpec((D_MODEL, MAIN_PROJ + SSD_HEADS), const, pipeline_mode=once),
            pl.BlockSpec((ATT_WIDTH + SSD_HEADS, D_MODEL), const, pipeline_mode=once),
        ],
        out_specs=(
            pl.BlockSpec((tm, ATT_WIDTH), row),
            pl.BlockSpec((ATT_WIDTH, tm), col),
            pl.BlockSpec((tm, ATT_WIDTH), row),
            pl.BlockSpec((tm, SSD_WIDTH), row),
            pl.BlockSpec((tm, CONV_DIM), row),
            pl.BlockSpec((SSD_HEADS, tm), col),
        ),
        compiler_params=pltpu.CompilerParams(
            dimension_semantics=("parallel",), vmem_limit_bytes=VMEM_LIMIT),
        name="in_proj",
    )(x2, norm_w, w_main, w_ktd)


ATT_SLOTS = ATT_BAND // ATT_TQ
ATT_NB = 4
ATT_RING = ATT_NB + ATT_SLOTS - 1
ATT_AHEAD = 2
PAIR = 2 * ATT_HEAD_DIM
LOG2E = 1.4426950408889634


def _bias_table_kernel(r_ref, o_ref):
    x = jnp.broadcast_to(r_ref[0], (ATT_TQ, ATT_BAND + ATT_TQ))
    rolled = pltpu.roll(x, 0, 1, stride=1, stride_axis=0)
    t = rolled[:, ATT_TQ:]
    qc = lax.broadcasted_iota(jnp.int32, (ATT_TQ, ATT_BAND), 0) // CHUNK
    kc = lax.broadcasted_iota(jnp.int32, (ATT_TQ, ATT_BAND), 1) // CHUNK
    valid = (kc >= qc) & (kc <= qc + N_PREV_CHUNKS)
    valid = valid & (pl.program_id(0) < ATT_HEADS)
    o_ref[0] = jnp.where(valid, t * LOG2E, NEG)


def _bias_table(rel_bias):
    rb = rel_bias.astype(F32).T
    n = ATT_BAND + ATT_TQ
    far = jnp.broadcast_to(rb[:, 2 * REL_CLIP:], (ATT_HEADS, n - 2 * REL_CLIP + 1))
    near = rb[:, 2 * REL_CLIP - 1:0:-1]
    r = jnp.concatenate([far, near], axis=1)
    r = jnp.concatenate([r, jnp.zeros((1, n), F32)], axis=0).reshape(ATT_HEADS + 1, 1, n)
    return pl.pallas_call(
        _bias_table_kernel,
        out_shape=jax.ShapeDtypeStruct((ATT_HEADS + 1, ATT_TQ, ATT_BAND), F32),
        grid=(ATT_HEADS + 1,),
        in_specs=[pl.BlockSpec((1, 1, n), lambda h: (h, 0, 0))],
        out_specs=pl.BlockSpec((1, ATT_TQ, ATT_BAND), lambda h: (h, 0, 0)),
        compiler_params=pltpu.CompilerParams(dimension_semantics=("parallel",)),
        name="bias_table",
    )(r)


def _attn_kernel(q_ref, kt_ref, v_ref, tab_ref, o_ref, kring, vring):
    i = pl.program_id(1)
    tq = ATT_TQ
    hd = ATT_HEAD_DIM
    lane = lax.broadcasted_iota(jnp.int32, (tq, PAIR), 1)
    low = lane < hd

    @pl.when(i == 0)
    def _():
        kring[...] = jnp.zeros(kring.shape, BF16)
        vring[...] = jnp.zeros(vring.shape, BF16)

    for j in range(ATT_NB):
        slot_new = lax.rem(ATT_NB * i + j, ATT_RING)
        cols = slice(j * tq, (j + 1) * tq)
        for h in range(ATT_HEADS):
            r0 = h * PAIR + (h % 2) * hd
            kring[slot_new, r0:r0 + hd, :] = kt_ref[h * hd:(h + 1) * hd, cols]
        for p in range(ATT_HEADS // 2):
            vp = v_ref[0, cols, p * PAIR:(p + 1) * PAIR]
            one = jnp.ones_like(vp)
            vring[slot_new, :, (2 * p) * PAIR:(2 * p + 1) * PAIR] = jnp.where(low, vp, one)
            vring[slot_new, :, (2 * p + 1) * PAIR:(2 * p + 2) * PAIR] = jnp.where(low, one, vp)

    def band(j):
        n = [ATT_NB * i + j - (ATT_SLOTS - 1) + m for m in range(ATT_SLOTS)]
        slots = [lax.rem(nm + ATT_RING, ATT_RING) for nm in n]
        masked = [nm < 0 for nm in n]
        return slots, masked

    bands = [band(j) for j in range(ATT_NB)]

    def scores(j, h):
        slots, masked = bands[j]
        p = h // 2
        qp = q_ref[0, j * tq:(j + 1) * tq, p * PAIR:(p + 1) * PAIR]
        return [jnp.dot(qp, kring[slots[m], h * PAIR:(h + 1) * PAIR, :],
                        preferred_element_type=F32)
                + tab_ref[jnp.where(masked[m], ATT_HEADS, h), :, m * tq:(m + 1) * tq]
                for m in range(ATT_SLOTS)]

    items = [(j, h) for j in range(ATT_NB) for h in range(ATT_HEADS)]
    queue = [scores(*it) for it in items[:ATT_AHEAD]]
    res = []
    for idx, (j, h) in enumerate(items):
        s = queue.pop(0)
        if idx + ATT_AHEAD < len(items):
            queue.append(scores(*items[idx + ATT_AHEAD]))
        slots, _ = bands[j]
        mx = functools.reduce(jnp.maximum, s)
        mx = jnp.max(mx, axis=-1, keepdims=True)
        acc = None
        for m in range(ATT_SLOTS):
            e = jnp.exp2(s[m] - mx).astype(BF16)
            pv = jnp.dot(e, vring[slots[m], :, h * PAIR:(h + 1) * PAIR],
                         preferred_element_type=F32)
            acc = pv if acc is None else acc + pv
        res.append(acc)
        if h % 2 == 1:
            p = h // 2
            num = jnp.where(low, res[0], res[1])
            den = jnp.where(low, pltpu.roll(res[0], hd, 1), pltpu.roll(res[1], hd, 1))
            o_ref[0, j * tq:(j + 1) * tq, p * PAIR:(p + 1) * PAIR] = (num / den).astype(BF16)
            res = []


def _split3(v):
    v1 = v.astype(BF16).astype(F32)
    r1 = v - v1
    v2 = r1.astype(BF16).astype(F32)
    v3 = (r1 - v2).astype(BF16).astype(F32)
    return v1, v2, v3


def _expand_heads(v, e3_ref):
    lane = lax.broadcasted_iota(jnp.int32, v.shape, 1)
    v = jnp.where(lane < SSD_HEADS, v, 0.0)
    v1, v2, v3 = _split3(v)
    packed = v1 + pltpu.roll(v2, SSD_HEADS, 1) + pltpu.roll(v3, 2 * SSD_HEADS, 1)
    return jnp.dot(packed.astype(BF16), e3_ref[...], preferred_element_type=F32)


CONV_TILE = 256
N_CONV_TILES = CONV_DIM // CONV_TILE


def _conv_tile(c, k, cw_ref, cb_ref, shift_ref, ubuf):
    L = SSD_L
    ch = slice(k * CONV_TILE, (k + 1) * CONV_TILE)
    win = ubuf[c * L:c * L + 2 * L, ch]
    sh = jnp.dot(shift_ref[...], win, preferred_element_type=F32)
    acc = cb_ref[:, ch] + cw_ref[SSD_CONV - 1:SSD_CONV, ch] * win[L:, :].astype(F32)
    for d in range(1, SSD_CONV):
        acc = acc + cw_ref[SSD_CONV - 1 - d:SSD_CONV - d, ch] * sh[(d - 1) * L:d * L, :]
    return _silu(acc)


def _decay_cumsum(c, dtt_ref, dtb_ref, alog_ref, tril3_ref):
    L = SSD_L
    rows = slice(c * L, (c + 1) * L)
    raw_t = jnp.concatenate([dtt_ref[:, rows], jnp.zeros((LANES - SSD_HEADS, L), F32)], axis=0)
    u = raw_t.T + dtb_ref[...]
    dt = jnp.maximum(u, 0.0) + jnp.log1p(jnp.exp(-jnp.abs(u)))
    a = dt * (-jnp.exp(alog_ref[...]) * LOG2E)
    a1, a2, a3 = _split3(a)
    a_parts = jnp.concatenate([a1, a2, a3], axis=0).astype(BF16)
    a_cum = jnp.dot(tril3_ref[...], a_parts, preferred_element_type=F32)
    return dt, a_cum


def _decay_weights(dt, a_cum, e3_ref):
    L = SSD_L
    a_last = a_cum[L - 1:L, :]
    w = dt * jnp.exp2(a_last - a_cum)
    w_exp = _expand_heads(w, e3_ref)
    cd = _expand_heads(jnp.broadcast_to(jnp.exp2(a_last), (SUBLANES, LANES)), e3_ref)[0:1, :]
    return w_exp, cd


def _ssd_front(tiles, dt, a_cum, w_exp, cd):
    xc = jnp.concatenate(tiles, axis=1)
    xs = xc[:, :SSD_WIDTH]
    xs_bf = xs.astype(BF16)
    bm = xc[:, SSD_WIDTH:SSD_WIDTH + SSD_GROUPS * SSD_STATE]
    cm = xc[:, SSD_WIDTH + SSD_GROUPS * SSD_STATE:]
    xd = (xs * w_exp).astype(BF16)
    return xs, xs_bf, bm, cm, a_cum, a_cum.T, dt.T, jnp.exp2(a_cum), xd, cd


def _ssd_back(c, front, gz_ref, dexp_ref, y_ref, state, causal, low, hooks):
    L = SSD_L
    gw = SSD_WIDTH // SSD_GROUPS
    ppg = SSD_HEADS // SSD_GROUPS // 2
    rows = slice(c * L, (c + 1) * L)
    xs, xs_bf, bm, cm, a_cum, a_cum_t, dt_t, exp_acum, xd, cd = front

    def finish(p, lhs, rhs):
        yp = jnp.dot(lhs, rhs, preferred_element_type=F32)
        yp = yp + dexp_ref[:, p * PAIR:(p + 1) * PAIR] * xs[:, p * PAIR:(p + 1) * PAIR]
        gz = gz_ref[0, rows, p * PAIR:(p + 1) * PAIR].astype(F32)
        y_ref[0, rows, p * PAIR:(p + 1) * PAIR] = (yp * gz).astype(BF16)

    zero = jnp.zeros((L, PAIR), BF16)
    pending = None
    for g in range(SSD_GROUPS):
        bg = bm[:, g * SSD_STATE:(g + 1) * SSD_STATE]
        cg = cm[:, g * SSD_STATE:(g + 1) * SSD_STATE]
        cb = lax.dot_general(cg.astype(BF16), bg.astype(BF16), (((1,), (1,)), ((), ())),
                             preferred_element_type=F32)
        sg = state[g]
        sg_bf = sg.astype(BF16)
        for q in range(ppg):
            p = g * ppg + q
            lhs = []
            for h in (2 * p, 2 * p + 1):
                seg = a_cum[:, h:h + 1] - a_cum_t[h:h + 1, :]
                dec = jnp.where(causal, jnp.exp2(seg), 0.0)
                lhs.append((cb * dec * dt_t[h:h + 1, :]).astype(BF16))
            for h in (2 * p, 2 * p + 1):
                lhs.append((cg * exp_acum[:, h:h + 1]).astype(BF16))
            xp = xs_bf[:, p * PAIR:(p + 1) * PAIR]
            sp = sg_bf[:, q * PAIR:(q + 1) * PAIR]
            rhs = jnp.concatenate([jnp.where(low, xp, zero), jnp.where(low, zero, xp),
                                   jnp.where(low, sp, zero), jnp.where(low, zero, sp)], axis=0)
            lhs = jnp.concatenate(lhs, axis=1)
            if pending is not None:
                finish(*pending)
            pending = (p, lhs, rhs)
            if p < len(hooks):
                hooks[p]()
        finish(*pending)
        pending = None
        s_new = jnp.dot(bg.T.astype(BF16), xd[:, g * gw:(g + 1) * gw],
                        preferred_element_type=F32)
        state[g] = sg * cd[:, g * gw:(g + 1) * gw] + s_new


def _ssd_kernel(gz_ref, xbc_ref, dtt_ref, cw_ref, cb_ref, dtb_ref, alog_ref,
                dexp_ref, e3_ref, tril3_ref, shift_ref, y_ref, ubuf, state):
    j = pl.program_id(1)
    T = SSD_T
    L = SSD_L

    @pl.when(j == 0)
    def _():
        ubuf[0:L, :] = jnp.zeros((L, CONV_DIM), BF16)
        state[...] = jnp.zeros(state.shape, F32)

    @pl.when(j > 0)
    def _():
        ubuf[0:L, :] = ubuf[T:T + L, :]

    ubuf[L:T + L, :] = xbc_ref[0]

    row = lax.broadcasted_iota(jnp.int32, (L, L), 0)
    col = lax.broadcasted_iota(jnp.int32, (L, L), 1)
    causal = row >= col
    low = lax.broadcasted_iota(jnp.int32, (L, PAIR), 1) < SSD_HEAD_DIM
    def conv(c, k):
        return _conv_tile(c, k, cw_ref, cb_ref, shift_ref, ubuf)

    def cumsum(c):
        return _decay_cumsum(c, dtt_ref, dtb_ref, alog_ref, tril3_ref)

    n = T // L
    dt, a_cum = cumsum(0)
    tiles = [conv(0, k) for k in range(N_CONV_TILES)]
    cur = _ssd_front(tiles, dt, a_cum, *_decay_weights(dt, a_cum, e3_ref))
    for c in range(n):
        nxt = {}
        hooks = []
        if c + 1 < n:
            def h_cumsum(cc=c + 1):
                nxt["dt"], nxt["a_cum"] = cumsum(cc)

            def h_conv(k, cc=c + 1):
                nxt.setdefault("tiles", []).append(conv(cc, k))

            def h_weights():
                nxt["w"] = _decay_weights(nxt["dt"], nxt["a_cum"], e3_ref)

            hooks = ([h_cumsum] + [functools.partial(h_conv, k) for k in range(N_CONV_TILES)]
                     + [h_weights])
        _ssd_back(c, cur, gz_ref, dexp_ref, y_ref, state, causal, low, hooks)
        if c + 1 < n:
            cur = _ssd_front(nxt["tiles"], nxt["dt"], nxt["a_cum"], *nxt["w"])


def _mixers(q, kt, v, table, gz, xbc, dtt_raw, conv_w, conv_b, dt_bias_p, a_log_p, d_exp,
            e3, tril3, shift):
    b, s, _ = gz.shape
    T = SSD_T
    assert T == ATT_NB * ATT_TQ
    nblk = s // T
    blk = lambda bi, j: (bi, j, 0)
    colblk = lambda bi, j: (0, bi * nblk + j)
    const = lambda bi, j: (0, 0)
    once = pl.Buffered(1)
    params = pltpu.CompilerParams(dimension_semantics=("parallel", "arbitrary"),
                                  vmem_limit_bytes=VMEM_LIMIT)
    att = pl.pallas_call(
        _attn_kernel,
        out_shape=jax.ShapeDtypeStruct((b, s, ATT_WIDTH), BF16),
        grid=(b, nblk),
        in_specs=[
            pl.BlockSpec((1, T, ATT_WIDTH), blk),
            pl.BlockSpec((ATT_WIDTH, T), colblk),
            pl.BlockSpec((1, T, ATT_WIDTH), blk),
            pl.BlockSpec((ATT_HEADS + 1, ATT_TQ, ATT_BAND), lambda bi, j: (0, 0, 0),
                         pipeline_mode=once),
        ],
        out_specs=pl.BlockSpec((1, T, ATT_WIDTH), blk),
        scratch_shapes=[
            pltpu.VMEM((ATT_RING, ATT_HEADS * PAIR, ATT_TQ), BF16),
            pltpu.VMEM((ATT_RING, ATT_TQ, ATT_HEADS * PAIR), BF16),
        ],
        compiler_params=params,
        name="band_attn",
    )(q, kt, v, table)
    ssd = pl.pallas_call(
        _ssd_kernel,
        out_shape=jax.ShapeDtypeStruct((b, s, SSD_WIDTH), BF16),
        grid=(b, nblk),
        in_specs=[
            pl.BlockSpec((1, T, SSD_WIDTH), blk),
            pl.BlockSpec((1, T, CONV_DIM), blk),
            pl.BlockSpec((SSD_HEADS, T), colblk),
            pl.BlockSpec((SSD_CONV, CONV_DIM), const),
            pl.BlockSpec((1, CONV_DIM), const),
            pl.BlockSpec((1, LANES), const),
            pl.BlockSpec((1, LANES), const),
            pl.BlockSpec((1, SSD_WIDTH), const),
            pl.BlockSpec((LANES, SSD_WIDTH), const, pipeline_mode=once),
            pl.BlockSpec((SSD_L, 3 * SSD_L), const, pipeline_mode=once),
            pl.BlockSpec(((SSD_CONV - 1) * SSD_L, 2 * SSD_L), const, pipeline_mode=once),
        ],
        out_specs=pl.BlockSpec((1, T, SSD_WIDTH), blk),
        scratch_shapes=[
            pltpu.VMEM((T + SSD_L, CONV_DIM), BF16),
            pltpu.VMEM((SSD_GROUPS, SSD_STATE, SSD_WIDTH // SSD_GROUPS), F32),
        ],
        compiler_params=params,
        name="ssd_scan",
    )(gz, xbc, dtt_raw, conv_w, conv_b, dt_bias_p, a_log_p, d_exp, e3, tril3, shift)
    return att, ssd


def _shift_matrix():
    r = jnp.arange((SSD_CONV - 1) * SSD_L)[:, None]
    c = jnp.arange(2 * SSD_L)[None, :]
    return (c == SSD_L + r % SSD_L - (r // SSD_L + 1)).astype(BF16)


def _tril3_matrix():
    r = jnp.arange(SSD_L)[:, None]
    c = jnp.arange(3 * SSD_L)[None, :] % SSD_L
    return (r >= c).astype(BF16)


def _expand_matrix():
    r = jnp.arange(LANES)[:, None]
    c = jnp.arange(SSD_WIDTH)[None, :]
    hit = (r < 3 * SSD_HEADS) & ((r % SSD_HEADS) == c // SSD_HEAD_DIM)
    return hit.astype(BF16)


def _out_ffn_kernel(x_ref, att_ref, ssd_ref, sw_ref, wo_ref, nw_ref,
                    w1_ref, w2_ref, fw_ref, o_ref):
    gw = SSD_WIDTH // SSD_GROUPS
    ssd = []
    for g in range(SSD_GROUPS):
        yg = ssd_ref[:, g * gw:(g + 1) * gw].astype(F32)
        ssd.append(_rms(yg, sw_ref[:, g * gw:(g + 1) * gw]).astype(BF16))
    ssd = jnp.concatenate(ssd, axis=1)
    h = (x_ref[...]
         + jnp.dot(att_ref[...], wo_ref[:ATT_WIDTH, :], preferred_element_type=F32)
         + jnp.dot(ssd, wo_ref[ATT_WIDTH:, :], preferred_element_type=F32))
    hn = _rms(h, nw_ref[...]).astype(BF16)
    hid = jnp.dot(hn, w1_ref[...], preferred_element_type=F32)
    hid = jnp.square(jnp.maximum(hid, 0.0)).astype(BF16)
    h = h + jnp.dot(hid, w2_ref[...], preferred_element_type=F32)
    o_ref[...] = _rms(h, fw_ref[...])


def _out_ffn(x2, att2, ssd2, ssd_norm_w, w_out, norm_w, w1, w2, final_w):
    m = x2.shape[0]
    tm = FFN_TM
    row = lambda i: (i, 0)
    const = lambda i: (0, 0)
    once = pl.Buffered(1)
    return pl.pallas_call(
        _out_ffn_kernel,
        out_shape=jax.ShapeDtypeStruct((m, D_MODEL), F32),
        grid=(m // tm,),
        in_specs=[
            pl.BlockSpec((tm, D_MODEL), row),
            pl.BlockSpec((tm, ATT_WIDTH), row),
            pl.BlockSpec((tm, SSD_WIDTH), row),
            pl.BlockSpec((1, SSD_WIDTH), const),
            pl.BlockSpec((ATT_WIDTH + SSD_WIDTH, D_MODEL), const, pipeline_mode=once),
            pl.BlockSpec((1, D_MODEL), const),
            pl.BlockSpec((D_MODEL, D_FF), const, pipeline_mode=once),
            pl.BlockSpec((D_FF, D_MODEL), const, pipeline_mode=once),
            pl.BlockSpec((1, D_MODEL), const),
        ],
        out_specs=pl.BlockSpec((tm, D_MODEL), row),
        compiler_params=pltpu.CompilerParams(
            dimension_semantics=("parallel",), vmem_limit_bytes=VMEM_LIMIT),
        name="out_ffn",
    )(x2, att2, ssd2, ssd_norm_w, w_out, norm_w, w1, w2, final_w)


def _layer(x, norm_mix_w, w_in, rel_bias, conv_w, conv_b, dt_bias, a_log, d_skip,
           ssd_norm_w, w_out, norm_mlp_w, w_ff1, w_ff2, out_norm_w):
    b, s, d = x.shape
    m = b * s
    x2 = x.reshape(m, d)

    scale = ATT_HEAD_DIM ** -0.5 * LOG2E
    col = jnp.arange(MAIN_PROJ + SSD_HEADS)[None, :]
    w_main = (w_in * jnp.where(col < ATT_WIDTH, scale, 1.0)).astype(BF16)
    w_ktd = jnp.concatenate([w_in[:, ATT_WIDTH:2 * ATT_WIDTH], w_in[:, MAIN_PROJ:]],
                            axis=1).T.astype(BF16)
    pad_h = lambda v: jnp.pad(v.astype(F32), (0, LANES - SSD_HEADS)).reshape(1, LANES)

    q, kt, v, gz, xbc, dtt_raw = _in_proj(x2, norm_mix_w.reshape(1, d), w_main, w_ktd)

    att, ssd = _mixers(q.reshape(b, s, -1), kt, v.reshape(b, s, -1), _bias_table(rel_bias),
                       gz.reshape(b, s, -1), xbc.reshape(b, s, -1), dtt_raw,
                       conv_w.astype(F32), conv_b.reshape(1, -1).astype(F32),
                       pad_h(dt_bias), pad_h(a_log),
                       jnp.repeat(d_skip.astype(F32), SSD_HEAD_DIM).reshape(1, -1),
                       _expand_matrix(), _tril3_matrix(), _shift_matrix())

    y = _out_ffn(x2, att.reshape(m, -1), ssd.reshape(m, -1),
                 ssd_norm_w.reshape(1, -1).astype(F32),
                 w_out.astype(BF16),
                 norm_mlp_w.reshape(1, d), w_ff1.astype(BF16), w_ff2.astype(BF16),
                 out_norm_w.reshape(1, d))
    return y.reshape(b, s, d)


def kernel(x, norm_mix_w, w_in, rel_bias, conv_w, conv_b, dt_bias, a_log, d_skip,
           ssd_norm_w, w_out, norm_mlp_w, w_ff1, w_ff2, norm_final_w):
    depth = w_in.shape[0]
    assert depth == 1, "final norm is fused into the single layer's last kernel"
    return _layer(x, norm_mix_w[0], w_in[0], rel_bias[0], conv_w[0], conv_b[0],
                  dt_bias[0], a_log[0], d_skip[0], ssd_norm_w[0], w_out[0],
                  norm_mlp_w[0], w_ff1[0], w_ff2[0], norm_final_w)
```

```python
import functools

import jax
import jax.numpy as jnp
from jax import lax
from jax.experimental import pallas as pl
from jax.experimental.pallas import tpu as pltpu

F32 = jnp.float32
BF16 = jnp.bfloat16

D_MODEL = 1024
CHUNK = 64
ATT_HEADS = 8
ATT_HEAD_DIM = 64
ATT_WIDTH = ATT_HEADS * ATT_HEAD_DIM
N_PREV_CHUNKS = 8
REL_CLIP = 256
SSD_HEAD_DIM = 64
SSD_WIDTH = D_MODEL
SSD_HEADS = SSD_WIDTH // SSD_HEAD_DIM
SSD_GROUPS = 2
SSD_STATE = 128
SSD_CONV = 4
CONV_DIM = SSD_WIDTH + 2 * SSD_GROUPS * SSD_STATE
MAIN_PROJ = 3 * ATT_WIDTH + SSD_WIDTH + CONV_DIM
D_FF = 4 * D_MODEL
EPS = 1e-5
NEG = -1e30

LANES = 128
SUBLANES = 8

PROJ_TM = 1024
ATT_TQ = 256
ATT_BAND = ATT_TQ + N_PREV_CHUNKS * CHUNK
SSD_L = 128
SSD_T = 1024
SSD_CHUNKS = SSD_T // SSD_L
assert SSD_CHUNKS * SSD_HEADS == LANES
FFN_TM = 1024
VMEM_LIMIT = 56 * 1024 * 1024


def _rms(x, w):
    ms = jnp.mean(x * x, axis=-1, keepdims=True)
    return x * lax.rsqrt(ms + EPS) * w


def _silu(x):
    h = 0.5 * x
    return h + h * jnp.tanh(h)


def _in_proj_kernel(x_ref, nw_ref, wm_ref, wkt_ref,
                    q_ref, kt_ref, v_ref, gz_ref, xbc_ref, dtt_ref):
    hn = _rms(x_ref[...], nw_ref[...]).astype(BF16)

    def seg(lo, hi):
        return jnp.dot(hn, wm_ref[:, lo:hi], preferred_element_type=F32).astype(BF16)

    a = ATT_WIDTH
    zlo = 3 * a
    xlo = zlo + SSD_WIDTH
    q_ref[...] = seg(0, a)
    ktd = lax.dot_general(wkt_ref[...], hn, (((1,), (1,)), ((), ())),
                          preferred_element_type=F32)
    kt_ref[...] = ktd[:a, :].astype(BF16)
    dtt_ref[...] = ktd[a:, :]
    v_ref[...] = seg(2 * a, 3 * a)
    gz_ref[...] = _silu(jnp.dot(hn, wm_ref[:, zlo:xlo],
                                preferred_element_type=F32)).astype(BF16)
    xbc_ref[...] = seg(xlo, MAIN_PROJ)


def _in_proj(x2, norm_w, w_main, w_ktd):
    m = x2.shape[0]
    tm = PROJ_TM
    row = lambda i: (i, 0)
    col = lambda i: (0, i)
    const = lambda i: (0, 0)
    once = pl.Buffered(1)
    out_shape = (
        jax.ShapeDtypeStruct((m, ATT_WIDTH), BF16),
        jax.ShapeDtypeStruct((ATT_WIDTH, m), BF16),
        jax.ShapeDtypeStruct((m, ATT_WIDTH), BF16),
        jax.ShapeDtypeStruct((m, SSD_WIDTH), BF16),
        jax.ShapeDtypeStruct((m, CONV_DIM), BF16),
        jax.ShapeDtypeStruct((SSD_HEADS, m), F32),
    )
    return pl.pallas_call(
        _in_proj_kernel,
        out_shape=out_shape,
        grid=(m // tm,),
        in_specs=[
            pl.BlockSpec((tm, D_MODEL), row),
            pl.BlockSpec((1, D_MODEL), const),
            pl.BlockSpec((D_MODEL, MAIN_PROJ + SSD_HEADS), const, pipeline_mode=once),
            pl.BlockSpec((ATT_WIDTH + SSD_HEADS, D_MODEL), const, pipeline_mode=once),
        ],
        out_specs=(
            pl.BlockSpec((tm, ATT_WIDTH), row),
            pl.BlockSpec((ATT_WIDTH, tm), col),
            pl.BlockSpec((tm, ATT_WIDTH), row),
            pl.BlockSpec((tm, SSD_WIDTH), row),
            pl.BlockSpec((tm, CONV_DIM), row),
            pl.BlockSpec((SSD_HEADS, tm), col),
        ),
        compiler_params=pltpu.CompilerParams(
            dimension_semantics=("parallel",), vmem_limit_bytes=VMEM_LIMIT),
        name="in_proj",
    )(x2, norm_w, w_main, w_ktd)


ATT_SLOTS = ATT_BAND // ATT_TQ
ATT_NB = 4
ATT_RING = ATT_NB + ATT_SLOTS - 1
ATT_AHEAD = 2
PAIR = 2 * ATT_HEAD_DIM
LOG2E = 1.4426950408889634


def _bias_table_kernel(r_ref, o_ref):
    x = jnp.broadcast_to(r_ref[0], (ATT_TQ, ATT_BAND + ATT_TQ))
    rolled = pltpu.roll(x, 0, 1, stride=1, stride_axis=0)
    t = rolled[:, ATT_TQ:]
    qc = lax.broadcasted_iota(jnp.int32, (ATT_TQ, ATT_BAND), 0) // CHUNK
    kc = lax.broadcasted_iota(jnp.int32, (ATT_TQ, ATT_BAND), 1) // CHUNK
    valid = (kc >= qc) & (kc <= qc + N_PREV_CHUNKS)
    valid = valid & (pl.program_id(0) < ATT_HEADS)
    o_ref[0] = jnp.where(valid, t * LOG2E, NEG)


def _bias_table(rel_bias):
    rb = rel_bias.astype(F32).T
    n = ATT_BAND + ATT_TQ
    far = jnp.broadcast_to(rb[:, 2 * REL_CLIP:], (ATT_HEADS, n - 2 * REL_CLIP + 1))
    near = rb[:, 2 * REL_CLIP - 1:0:-1]
    r = jnp.concatenate([far, near], axis=1)
    r = jnp.concatenate([r, jnp.zeros((1, n), F32)], axis=0).reshape(ATT_HEADS + 1, 1, n)
    return pl.pallas_call(
        _bias_table_kernel,
        out_shape=jax.ShapeDtypeStruct((ATT_HEADS + 1, ATT_TQ, ATT_BAND), F32),
        grid=(ATT_HEADS + 1,),
        in_specs=[pl.BlockSpec((1, 1, n), lambda h: (h, 0, 0))],
        out_specs=pl.BlockSpec((1, ATT_TQ, ATT_BAND), lambda h: (h, 0, 0)),
        compiler_params=pltpu.CompilerParams(dimension_semantics=("parallel",)),
        name="bias_table",
    )(r)


def _attn_kernel(q_ref, kt_ref, v_ref, tab_ref, o_ref, kring, vring):
    i = pl.program_id(1)
    tq = ATT_TQ
    hd = ATT_HEAD_DIM
    lane = lax.broadcasted_iota(jnp.int32, (tq, PAIR), 1)
    low = lane < hd

    @pl.when(i == 0)
    def _():
        kring[...] = jnp.zeros(kring.shape, BF16)
        vring[...] = jnp.zeros(vring.shape, BF16)

    for j in range(ATT_NB):
        slot_new = lax.rem(ATT_NB * i + j, ATT_RING)
        cols = slice(j * tq, (j + 1) * tq)
        for h in range(ATT_HEADS):
            r0 = h * PAIR + (h % 2) * hd
            kring[slot_new, r0:r0 + hd, :] = kt_ref[h * hd:(h + 1) * hd, cols]
        for p in range(ATT_HEADS // 2):
            vp = v_ref[0, cols, p * PAIR:(p + 1) * PAIR]
            one = jnp.ones_like(vp)
            vring[slot_new, :, (2 * p) * PAIR:(2 * p + 1) * PAIR] = jnp.where(low, vp, one)
            vring[slot_new, :, (2 * p + 1) * PAIR:(2 * p + 2) * PAIR] = jnp.where(low, one, vp)

    def band(j):
        n = [ATT_NB * i + j - (ATT_SLOTS - 1) + m for m in range(ATT_SLOTS)]
        slots = [lax.rem(nm + ATT_RING, ATT_RING) for nm in n]
        masked = [nm < 0 for nm in n]
        return slots, masked

    bands = [band(j) for j in range(ATT_NB)]

    def scores(j, h):
        slots, masked = bands[j]
        p = h // 2
        qp = q_ref[0, j * tq:(j + 1) * tq, p * PAIR:(p + 1) * PAIR]
        return [jnp.dot(qp, kring[slots[m], h * PAIR:(h + 1) * PAIR, :],
                        preferred_element_type=F32)
                + tab_ref[jnp.where(masked[m], ATT_HEADS, h), :, m * tq:(m + 1) * tq]
                for m in range(ATT_SLOTS)]

    items = [(j, h) for j in range(ATT_NB) for h in range(ATT_HEADS)]
    queue = [scores(*it) for it in items[:ATT_AHEAD]]
    res = []
    for idx, (j, h) in enumerate(items):
        s = queue.pop(0)
        if idx + ATT_AHEAD < len(items):
            queue.append(scores(*items[idx + ATT_AHEAD]))
        slots, _ = bands[j]
        mx = functools.reduce(jnp.maximum, s)
        mx = jnp.max(mx, axis=-1, keepdims=True)
        acc = None
        for m in range(ATT_SLOTS):
            e = jnp.exp2(s[m] - mx).astype(BF16)
            pv = jnp.dot(e, vring[slots[m], :, h * PAIR:(h + 1) * PAIR],
                         preferred_element_type=F32)
            acc = pv if acc is None else acc + pv
        res.append(acc)
        if h % 2 == 1:
            p = h // 2
            num = jnp.where(low, res[0], res[1])
            den = jnp.where(low, pltpu.roll(res[0], hd, 1), pltpu.roll(res[1], hd, 1))
            o_ref[0, j * tq:(j + 1) * tq, p * PAIR:(p + 1) * PAIR] = (num / den).astype(BF16)
            res = []


def _split3(v):
    v1 = v.astype(BF16).astype(F32)
    r1 = v - v1
    v2 = r1.astype(BF16).astype(F32)
    v3 = (r1 - v2).astype(BF16).astype(F32)
    return v1, v2, v3


CONV_TILE = 256
N_CONV_TILES = CONV_DIM // CONV_TILE


def _conv_tile(c, k, cw_ref, cb_ref, shift_ref, ubuf):
    L = SSD_L
    ch = slice(k * CONV_TILE, (k + 1) * CONV_TILE)
    win = ubuf[c * L:c * L + 2 * L, ch]
    sh = jnp.dot(shift_ref[...], win, preferred_element_type=F32)
    acc = cb_ref[:, ch] + cw_ref[SSD_CONV - 1:SSD_CONV, ch] * win[L:, :].astype(F32)
    for d in range(1, SSD_CONV):
        acc = acc + cw_ref[SSD_CONV - 1 - d:SSD_CONV - d, ch] * sh[(d - 1) * L:d * L, :]
    return _silu(acc)


def _decay_terms(dtt_ref, dtb_ref, alog_ref, tril3_ref):
    L = SSD_L
    raw_t = jnp.concatenate([dtt_ref[:, c * L:(c + 1) * L] for c in range(SSD_CHUNKS)], axis=0)
    u = raw_t.T + dtb_ref[...]
    dt = jnp.maximum(u, 0.0) + jnp.log1p(jnp.exp(-jnp.abs(u)))
    a = dt * (-jnp.exp(alog_ref[...]) * LOG2E)
    a_parts = jnp.concatenate(_split3(a), axis=0).astype(BF16)
    a_cum = jnp.dot(tril3_ref[...], a_parts, preferred_element_type=F32)
    a_last = a_cum[L - 1:L, :]
    w = dt * jnp.exp2(a_last - a_cum)
    cd = jnp.broadcast_to(jnp.exp2(a_last), (SUBLANES, LANES))
    packed = jnp.concatenate(_split3(jnp.concatenate([w, cd], axis=0)), axis=1).astype(BF16)
    return dict(a_cum=a_cum, a_cum_t=a_cum.T, dt_t=dt.T, exp_acum=jnp.exp2(a_cum),
                packed=packed)


def _decay_weights(c, terms, e3c_ref):
    ex = jnp.dot(terms["packed"], e3c_ref[c], preferred_element_type=F32)
    return ex[:SSD_L, :], ex[SSD_L:SSD_L + 1, :]


def _ssd_front(tiles, w_exp, cd):
    xc = jnp.concatenate(tiles, axis=1)
    xs = xc[:, :SSD_WIDTH]
    xs_bf = xs.astype(BF16)
    bm = xc[:, SSD_WIDTH:SSD_WIDTH + SSD_GROUPS * SSD_STATE]
    cm = xc[:, SSD_WIDTH + SSD_GROUPS * SSD_STATE:]
    xd = (xs * w_exp).astype(BF16)
    return xs, xs_bf, bm, cm, xd, cd


def _ssd_back(c, front, terms, gz_ref, dexp_ref, y_ref, state, causal, low, hooks):
    L = SSD_L
    gw = SSD_WIDTH // SSD_GROUPS
    ppg = SSD_HEADS // SSD_GROUPS // 2
    rows = slice(c * L, (c + 1) * L)
    xs, xs_bf, bm, cm, xd, cd = front
    a_cum, a_cum_t, dt_t, exp_acum = (terms[k] for k in ("a_cum", "a_cum_t", "dt_t", "exp_acum"))
    lane0 = c * SSD_HEADS

    def finish(p, lhs, rhs):
        yp = jnp.dot(lhs, rhs, preferred_element_type=F32)
        yp = yp + dexp_ref[:, p * PAIR:(p + 1) * PAIR] * xs[:, p * PAIR:(p + 1) * PAIR]
        gz = gz_ref[0, rows, p * PAIR:(p + 1) * PAIR].astype(F32)
        y_ref[0, rows, p * PAIR:(p + 1) * PAIR] = (yp * gz).astype(BF16)

    zero = jnp.zeros((L, PAIR), BF16)
    pending = None
    for g in range(SSD_GROUPS):
        bg = bm[:, g * SSD_STATE:(g + 1) * SSD_STATE]
        cg = cm[:, g * SSD_STATE:(g + 1) * SSD_STATE]
        cb = lax.dot_general(cg.astype(BF16), bg.astype(BF16), (((1,), (1,)), ((), ())),
                             preferred_element_type=F32)
        sg = state[g]
        sg_bf = sg.astype(BF16)
        for q in range(ppg):
            p = g * ppg + q
            lhs = []
            for h in (lane0 + 2 * p, lane0 + 2 * p + 1):
                seg = a_cum[:, h:h + 1] - a_cum_t[h:h + 1, :]
                dec = jnp.where(causal, jnp.exp2(seg), 0.0)
                lhs.append((cb * dec * dt_t[h:h + 1, :]).astype(BF16))
            for h in (lane0 + 2 * p, lane0 + 2 * p + 1):
                lhs.append((cg * exp_acum[:, h:h + 1]).astype(BF16))
            xp = xs_bf[:, p * PAIR:(p + 1) * PAIR]
            sp = sg_bf[:, q * PAIR:(q + 1) * PAIR]
            rhs = jnp.concatenate([jnp.where(low, xp, zero), jnp.where(low, zero, xp),
                                   jnp.where(low, sp, zero), jnp.where(low, zero, sp)], axis=0)
            lhs = jnp.concatenate(lhs, axis=1)
            if pending is not None:
                finish(*pending)
            pending = (p, lhs, rhs)
            if p < len(hooks):
                hooks[p]()
        finish(*pending)
        pending = None
        s_new = jnp.dot(bg.T.astype(BF16), xd[:, g * gw:(g + 1) * gw],
                        preferred_element_type=F32)
        state[g] = sg * cd[:, g * gw:(g + 1) * gw] + s_new


def _ssd_kernel(gz_ref, xbc_ref, dtt_ref, cw_ref, cb_ref, dtb_ref, alog_ref,
                dexp_ref, e3c_ref, tril3_ref, shift_ref, y_ref, ubuf, state):
    j = pl.program_id(1)
    T = SSD_T
    L = SSD_L

    @pl.when(j == 0)
    def _():
        ubuf[0:L, :] = jnp.zeros((L, CONV_DIM), BF16)
        state[...] = jnp.zeros(state.shape, F32)

    @pl.when(j > 0)
    def _():
        ubuf[0:L, :] = ubuf[T:T + L, :]

    ubuf[L:T + L, :] = xbc_ref[0]

    row = lax.broadcasted_iota(jnp.int32, (L, L), 0)
    col = lax.broadcasted_iota(jnp.int32, (L, L), 1)
    causal = row >= col
    low = lax.broadcasted_iota(jnp.int32, (L, PAIR), 1) < SSD_HEAD_DIM
    def conv(c, k):
        return _conv_tile(c, k, cw_ref, cb_ref, shift_ref, ubuf)

    terms = _decay_terms(dtt_ref, dtb_ref, alog_ref, tril3_ref)
    n = SSD_CHUNKS
    cur = _ssd_front([conv(0, k) for k in range(N_CONV_TILES)],
                     *_decay_weights(0, terms, e3c_ref))
    for c in range(n):
        nxt = {}
        hooks = []
        if c + 1 < n:
            def h_conv(k, cc=c + 1):
                nxt.setdefault("tiles", []).append(conv(cc, k))

            def h_weights(cc=c + 1):
                nxt["w"] = _decay_weights(cc, terms, e3c_ref)

            hooks = [functools.partial(h_conv, k) for k in range(N_CONV_TILES)] + [h_weights]
        _ssd_back(c, cur, terms, gz_ref, dexp_ref, y_ref, state, causal, low, hooks)
        if c + 1 < n:
            cur = _ssd_front(nxt["tiles"], *nxt["w"])


def _mixers(q, kt, v, table, gz, xbc, dtt_raw, conv_w, conv_b, dt_bias_p, a_log_p, d_exp,
            e3c, tril3, shift):
    b, s, _ = gz.shape
    T = SSD_T
    assert T == ATT_NB * ATT_TQ
    nblk = s // T
    blk = lambda bi, j: (bi, j, 0)
    colblk = lambda bi, j: (0, bi * nblk + j)
    const = lambda bi, j: (0, 0)
    once = pl.Buffered(1)
    params = pltpu.CompilerParams(dimension_semantics=("parallel", "arbitrary"),
                                  vmem_limit_bytes=VMEM_LIMIT)
    att = pl.pallas_call(
        _attn_kernel,
        out_shape=jax.ShapeDtypeStruct((b, s, ATT_WIDTH), BF16),
        grid=(b, nblk),
        in_specs=[
            pl.BlockSpec((1, T, ATT_WIDTH), blk),
            pl.BlockSpec((ATT_WIDTH, T), colblk),
            pl.BlockSpec((1, T, ATT_WIDTH), blk),
            pl.BlockSpec((ATT_HEADS + 1, ATT_TQ, ATT_BAND), lambda bi, j: (0, 0, 0),
                         pipeline_mode=once),
        ],
        out_specs=pl.BlockSpec((1, T, ATT_WIDTH), blk),
        scratch_shapes=[
            pltpu.VMEM((ATT_RING, ATT_HEADS * PAIR, ATT_TQ), BF16),
            pltpu.VMEM((ATT_RING, ATT_TQ, ATT_HEADS * PAIR), BF16),
        ],
        compiler_params=params,
        name="band_attn",
    )(q, kt, v, table)
    ssd = pl.pallas_call(
        _ssd_kernel,
        out_shape=jax.ShapeDtypeStruct((b, s, SSD_WIDTH), BF16),
        grid=(b, nblk),
        in_specs=[
            pl.BlockSpec((1, T, SSD_WIDTH), blk),
            pl.BlockSpec((1, T, CONV_DIM), blk),
            pl.BlockSpec((SSD_HEADS, T), colblk),
            pl.BlockSpec((SSD_CONV, CONV_DIM), const),
            pl.BlockSpec((1, CONV_DIM), const),
            pl.BlockSpec((1, LANES), const),
            pl.BlockSpec((1, LANES), const),
            pl.BlockSpec((1, SSD_WIDTH), const),
            pl.BlockSpec((SSD_CHUNKS, 3 * LANES, SSD_WIDTH), lambda bi, j: (0, 0, 0),
                         pipeline_mode=once),
            pl.BlockSpec((SSD_L, 3 * SSD_L), const, pipeline_mode=once),
            pl.BlockSpec(((SSD_CONV - 1) * SSD_L, 2 * SSD_L), const, pipeline_mode=once),
        ],
        out_specs=pl.BlockSpec((1, T, SSD_WIDTH), blk),
        scratch_shapes=[
            pltpu.VMEM((T + SSD_L, CONV_DIM), BF16),
            pltpu.VMEM((SSD_GROUPS, SSD_STATE, SSD_WIDTH // SSD_GROUPS), F32),
        ],
        compiler_params=params,
        name="ssd_scan",
    )(gz, xbc, dtt_raw, conv_w, conv_b, dt_bias_p, a_log_p, d_exp, e3c, tril3, shift)
    return att, ssd


def _shift_matrix():
    r = jnp.arange((SSD_CONV - 1) * SSD_L)[:, None]
    c = jnp.arange(2 * SSD_L)[None, :]
    return (c == SSD_L + r % SSD_L - (r // SSD_L + 1)).astype(BF16)


def _tril3_matrix():
    r = jnp.arange(SSD_L)[:, None]
    c = jnp.arange(3 * SSD_L)[None, :] % SSD_L
    return (r >= c).astype(BF16)


def _expand_matrices():
    r = jnp.arange(3 * LANES)[None, :, None] % LANES
    c = jnp.arange(SSD_CHUNKS)[:, None, None]
    col = jnp.arange(SSD_WIDTH)[None, None, :]
    hit = (r // SSD_HEADS == c) & (r % SSD_HEADS == col // SSD_HEAD_DIM)
    return hit.astype(BF16)


def _out_ffn_kernel(x_ref, att_ref, ssd_ref, sw_ref, wo_ref, nw_ref,
                    w1_ref, w2_ref, fw_ref, o_ref):
    gw = SSD_WIDTH // SSD_GROUPS
    ssd = []
    for g in range(SSD_GROUPS):
        yg = ssd_ref[:, g * gw:(g + 1) * gw].astype(F32)
        ssd.append(_rms(yg, sw_ref[:, g * gw:(g + 1) * gw]).astype(BF16))
    ssd = jnp.concatenate(ssd, axis=1)
    h = (x_ref[...]
         + jnp.dot(att_ref[...], wo_ref[:ATT_WIDTH, :], preferred_element_type=F32)
         + jnp.dot(ssd, wo_ref[ATT_WIDTH:, :], preferred_element_type=F32))
    hn = _rms(h, nw_ref[...]).astype(BF16)
    hid = jnp.dot(hn, w1_ref[...], preferred_element_type=F32)
    hid = jnp.square(jnp.maximum(hid, 0.0)).astype(BF16)
    h = h + jnp.dot(hid, w2_ref[...], preferred_element_type=F32)
    o_ref[...] = _rms(h, fw_ref[...])


def _out_ffn(x2, att2, ssd2, ssd_norm_w, w_out, norm_w, w1, w2, final_w):
    m = x2.shape[0]
    tm = FFN_TM
    row = lambda i: (i, 0)
    const = lambda i: (0, 0)
    once = pl.Buffered(1)
    return pl.pallas_call(
        _out_ffn_kernel,
        out_shape=jax.ShapeDtypeStruct((m, D_MODEL), F32),
        grid=(m // tm,),
        in_specs=[
            pl.BlockSpec((tm, D_MODEL), row),
            pl.BlockSpec((tm, ATT_WIDTH), row),
            pl.BlockSpec((tm, SSD_WIDTH), row),
            pl.BlockSpec((1, SSD_WIDTH), const),
            pl.BlockSpec((ATT_WIDTH + SSD_WIDTH, D_MODEL), const, pipeline_mode=once),
            pl.BlockSpec((1, D_MODEL), const),
            pl.BlockSpec((D_MODEL, D_FF), const, pipeline_mode=once),
            pl.BlockSpec((D_FF, D_MODEL), const, pipeline_mode=once),
            pl.BlockSpec((1, D_MODEL), const),
        ],
        out_specs=pl.BlockSpec((tm, D_MODEL), row),
        compiler_params=pltpu.CompilerParams(
            dimension_semantics=("parallel",), vmem_limit_bytes=VMEM_LIMIT),
        name="out_ffn",
    )(x2, att2, ssd2, ssd_norm_w, w_out, norm_w, w1, w2, final_w)


def _layer(x, norm_mix_w, w_in, rel_bias, conv_w, conv_b, dt_bias, a_log, d_skip,
           ssd_norm_w, w_out, norm_mlp_w, w_ff1, w_ff2, out_norm_w):
    b, s, d = x.shape
    m = b * s
    x2 = x.reshape(m, d)

    scale = ATT_HEAD_DIM ** -0.5 * LOG2E
    col = jnp.arange(MAIN_PROJ + SSD_HEADS)[None, :]
    w_main = (w_in * jnp.where(col < ATT_WIDTH, scale, 1.0)).astype(BF16)
    w_ktd = jnp.concatenate([w_in[:, ATT_WIDTH:2 * ATT_WIDTH], w_in[:, MAIN_PROJ:]],
                            axis=1).T.astype(BF16)
    per_lane = lambda v: jnp.tile(v.astype(F32), SSD_CHUNKS).reshape(1, LANES)

    q, kt, v, gz, xbc, dtt_raw = _in_proj(x2, norm_mix_w.reshape(1, d), w_main, w_ktd)

    att, ssd = _mixers(q.reshape(b, s, -1), kt, v.reshape(b, s, -1), _bias_table(rel_bias),
                       gz.reshape(b, s, -1), xbc.reshape(b, s, -1), dtt_raw,
                       conv_w.astype(F32), conv_b.reshape(1, -1).astype(F32),
                       per_lane(dt_bias), per_lane(a_log),
                       jnp.repeat(d_skip.astype(F32), SSD_HEAD_DIM).reshape(1, -1),
                       _expand_matrices(), _tril3_matrix(), _shift_matrix())

    y = _out_ffn(x2, att.reshape(m, -1), ssd.reshape(m, -1),
                 ssd_norm_w.reshape(1, -1).astype(F32),
                 w_out.astype(BF16),
                 norm_mlp_w.reshape(1, d), w_ff1.astype(BF16), w_ff2.astype(BF16),
                 out_norm_w.reshape(1, d))
    return y.reshape(b, s, d)


def kernel(x, norm_mix_w, w_in, rel_bias, conv_w, conv_b, dt_bias, a_log, d_skip,
           ssd_norm_w, w_out, norm_mlp_w, w_ff1, w_ff2, norm_final_w):
    depth = w_in.shape[0]
    assert depth == 1, "final norm is fused into the single layer's last kernel"
    return _layer(x, norm_mix_w[0], w_in[0], rel_bias[0], conv_w[0], conv_b[0],
                  dt_bias[0], a_log[0], d_skip[0], ssd_norm_w[0], w_out[0],
                  norm_mlp_w[0], w_ff1[0], w_ff2[0], norm_final_w)
```

```python
import functools

import jax
import jax.numpy as jnp
from jax import lax
from jax.experimental import pallas as pl
from jax.experimental.pallas import tpu as pltpu

F32 = jnp.float32
BF16 = jnp.bfloat16

D_MODEL = 1024
CHUNK = 64
ATT_HEADS = 8
ATT_HEAD_DIM = 64
ATT_WIDTH = ATT_HEADS * ATT_HEAD_DIM
N_PREV_CHUNKS = 8
REL_CLIP = 256
SSD_HEAD_DIM = 64
SSD_WIDTH = D_MODEL
SSD_HEADS = SSD_WIDTH // SSD_HEAD_DIM
SSD_GROUPS = 2
SSD_STATE = 128
SSD_CONV = 4
CONV_DIM = SSD_WIDTH + 2 * SSD_GROUPS * SSD_STATE
MAIN_PROJ = 3 * ATT_WIDTH + SSD_WIDTH + CONV_DIM
D_FF = 4 * D_MODEL
EPS = 1e-5
NEG = -1e30

LANES = 128
SUBLANES = 8

PROJ_TM = 1024
ATT_TQ = 256
ATT_BAND = ATT_TQ + N_PREV_CHUNKS * CHUNK
SSD_L = 128
SSD_T = 1024
SSD_CHUNKS = SSD_T // SSD_L
assert SSD_CHUNKS * SSD_HEADS == LANES
FFN_TM = 1024
VMEM_LIMIT = 56 * 1024 * 1024


def _rms(x, w):
    ms = jnp.mean(x * x, axis=-1, keepdims=True)
    return x * lax.rsqrt(ms + EPS) * w


def _silu(x):
    h = 0.5 * x
    return h + h * jnp.tanh(h)


def _in_proj_kernel(x_ref, nw_ref, wm_ref, wkt_ref,
                    q_ref, kt_ref, v_ref, gz_ref, xbc_ref, dtt_ref):
    hn = _rms(x_ref[...], nw_ref[...]).astype(BF16)

    def seg(lo, hi):
        return jnp.dot(hn, wm_ref[:, lo:hi], preferred_element_type=F32).astype(BF16)

    a = ATT_WIDTH
    zlo = 3 * a
    xlo = zlo + SSD_WIDTH
    q_ref[...] = seg(0, a)
    ktd = lax.dot_general(wkt_ref[...], hn, (((1,), (1,)), ((), ())),
                          preferred_element_type=F32)
    kt_ref[...] = ktd[:a, :].astype(BF16)
    dtt_ref[...] = ktd[a:, :]
    v_ref[...] = seg(2 * a, 3 * a)
    gz_ref[...] = _silu(jnp.dot(hn, wm_ref[:, zlo:xlo],
                                preferred_element_type=F32)).astype(BF16)
    xbc_ref[...] = seg(xlo, MAIN_PROJ)


def _in_proj(x2, norm_w, w_main, w_ktd):
    m = x2.shape[0]
    tm = PROJ_TM
    row = lambda i: (i, 0)
    col = lambda i: (0, i)
    const = lambda i: (0, 0)
    once = pl.Buffered(1)
    out_shape = (
        jax.ShapeDtypeStruct((m, ATT_WIDTH), BF16),
        jax.ShapeDtypeStruct((ATT_WIDTH, m), BF16),
        jax.ShapeDtypeStruct((m, ATT_WIDTH), BF16),
        jax.ShapeDtypeStruct((m, SSD_WIDTH), BF16),
        jax.ShapeDtypeStruct((m, CONV_DIM), BF16),
        jax.ShapeDtypeStruct((SSD_HEADS, m), F32),
    )
    return pl.pallas_call(
        _in_proj_kernel,
        out_shape=out_shape,
        grid=(m // tm,),
        in_specs=[
            pl.BlockSpec((tm, D_MODEL), row),
            pl.BlockSpec((1, D_MODEL), const),
            pl.BlockSpec((D_MODEL, MAIN_PROJ + SSD_HEADS), const, pipeline_mode=once),
            pl.BlockSpec((ATT_WIDTH + SSD_HEADS, D_MODEL), const, pipeline_mode=once),
        ],
        out_specs=(
            pl.BlockSpec((tm, ATT_WIDTH), row),
            pl.BlockSpec((ATT_WIDTH, tm), col),
            pl.BlockSpec((tm, ATT_WIDTH), row),
            pl.BlockSpec((tm, SSD_WIDTH), row),
            pl.BlockSpec((tm, CONV_DIM), row),
            pl.BlockSpec((SSD_HEADS, tm), col),
        ),
        compiler_params=pltpu.CompilerParams(
            dimension_semantics=("parallel",), vmem_limit_bytes=VMEM_LIMIT),
        name="in_proj",
    )(x2, norm_w, w_main, w_ktd)


ATT_SLOTS = ATT_BAND // ATT_TQ
ATT_NB = 4
ATT_RING = ATT_NB + ATT_SLOTS - 1
ATT_AHEAD = 2
PAIR = 2 * ATT_HEAD_DIM
LOG2E = 1.4426950408889634


def _bias_table_kernel(r_ref, o_ref):
    x = jnp.broadcast_to(r_ref[0], (ATT_TQ, ATT_BAND + ATT_TQ))
    rolled = pltpu.roll(x, 0, 1, stride=1, stride_axis=0)
    t = rolled[:, ATT_TQ:]
    qc = lax.broadcasted_iota(jnp.int32, (ATT_TQ, ATT_BAND), 0) // CHUNK
    kc = lax.broadcasted_iota(jnp.int32, (ATT_TQ, ATT_BAND), 1) // CHUNK
    valid = (kc >= qc) & (kc <= qc + N_PREV_CHUNKS)
    valid = valid & (pl.program_id(0) < ATT_HEADS)
    o_ref[0] = jnp.where(valid, t * LOG2E, NEG)


def _bias_table(rel_bias):
    rb = rel_bias.astype(F32).T
    n = ATT_BAND + ATT_TQ
    far = jnp.broadcast_to(rb[:, 2 * REL_CLIP:], (ATT_HEADS, n - 2 * REL_CLIP + 1))
    near = rb[:, 2 * REL_CLIP - 1:0:-1]
    r = jnp.concatenate([far, near], axis=1)
    r = jnp.concatenate([r, jnp.zeros((1, n), F32)], axis=0).reshape(ATT_HEADS + 1, 1, n)
    return pl.pallas_call(
        _bias_table_kernel,
        out_shape=jax.ShapeDtypeStruct((ATT_HEADS + 1, ATT_TQ, ATT_BAND), F32),
        grid=(ATT_HEADS + 1,),
        in_specs=[pl.BlockSpec((1, 1, n), lambda h: (h, 0, 0))],
        out_specs=pl.BlockSpec((1, ATT_TQ, ATT_BAND), lambda h: (h, 0, 0)),
        compiler_params=pltpu.CompilerParams(dimension_semantics=("parallel",)),
        name="bias_table",
    )(r)


def _attn_kernel(q_ref, kt_ref, v_ref, tab_ref, o_ref, kring, vring):
    i = pl.program_id(1)
    tq = ATT_TQ
    hd = ATT_HEAD_DIM
    lane = lax.broadcasted_iota(jnp.int32, (tq, PAIR), 1)
    low = lane < hd

    @pl.when(i == 0)
    def _():
        kring[...] = jnp.zeros(kring.shape, BF16)
        vring[...] = jnp.zeros(vring.shape, BF16)

    for j in range(ATT_NB):
        slot_new = lax.rem(ATT_NB * i + j, ATT_RING)
        cols = slice(j * tq, (j + 1) * tq)
        for h in range(ATT_HEADS):
            r0 = h * PAIR + (h % 2) * hd
            kring[slot_new, r0:r0 + hd, :] = kt_ref[h * hd:(h + 1) * hd, cols]
        for p in range(ATT_HEADS // 2):
            vp = v_ref[0, cols, p * PAIR:(p + 1) * PAIR]
            one = jnp.ones_like(vp)
            vring[slot_new, :, (2 * p) * PAIR:(2 * p + 1) * PAIR] = jnp.where(low, vp, one)
            vring[slot_new, :, (2 * p + 1) * PAIR:(2 * p + 2) * PAIR] = jnp.where(low, one, vp)

    def band(j):
        n = [ATT_NB * i + j - (ATT_SLOTS - 1) + m for m in range(ATT_SLOTS)]
        slots = [lax.rem(nm + ATT_RING, ATT_RING) for nm in n]
        masked = [nm < 0 for nm in n]
        return slots, masked

    bands = [band(j) for j in range(ATT_NB)]

    def scores(j, h):
        slots, masked = bands[j]
        p = h // 2
        qp = q_ref[0, j * tq:(j + 1) * tq, p * PAIR:(p + 1) * PAIR]
        return [jnp.dot(qp, kring[slots[m], h * PAIR:(h + 1) * PAIR, :],
                        preferred_element_type=F32)
                + tab_ref[jnp.where(masked[m], ATT_HEADS, h), :, m * tq:(m + 1) * tq]
                for m in range(ATT_SLOTS)]

    items = [(j, h) for j in range(ATT_NB) for h in range(ATT_HEADS)]
    queue = [scores(*it) for it in items[:ATT_AHEAD]]
    res = []
    for idx, (j, h) in enumerate(items):
        s = queue.pop(0)
        if idx + ATT_AHEAD < len(items):
            queue.append(scores(*items[idx + ATT_AHEAD]))
        slots, _ = bands[j]
        mx = functools.reduce(jnp.maximum, s)
        mx = jnp.max(mx, axis=-1, keepdims=True)
        acc = None
        for m in range(ATT_SLOTS):
            e = jnp.exp2(s[m] - mx).astype(BF16)
            pv = jnp.dot(e, vring[slots[m], :, h * PAIR:(h + 1) * PAIR],
                         preferred_element_type=F32)
            acc = pv if acc is None else acc + pv
        res.append(acc)
        if h % 2 == 1:
            p = h // 2
            num = jnp.where(low, res[0], res[1])
            den = jnp.where(low, pltpu.roll(res[0], hd, 1), pltpu.roll(res[1], hd, 1))
            o_ref[0, j * tq:(j + 1) * tq, p * PAIR:(p + 1) * PAIR] = (num / den).astype(BF16)
            res = []


def _split3(v):
    v1 = v.astype(BF16).astype(F32)
    r1 = v - v1
    v2 = r1.astype(BF16).astype(F32)
    v3 = (r1 - v2).astype(BF16).astype(F32)
    return v1, v2, v3


CONV_TILE = 256
N_CONV_TILES = CONV_DIM // CONV_TILE


def _conv_tile(c, k, cw_ref, cb_ref, shift_ref, ubuf):
    L = SSD_L
    ch = slice(k * CONV_TILE, (k + 1) * CONV_TILE)
    win = ubuf[c * L:c * L + 2 * L, ch]
    sh = jnp.dot(shift_ref[...], win, preferred_element_type=F32)
    acc = cb_ref[:, ch] + cw_ref[SSD_CONV - 1:SSD_CONV, ch] * win[L:, :].astype(F32)
    for d in range(1, SSD_CONV):
        acc = acc + cw_ref[SSD_CONV - 1 - d:SSD_CONV - d, ch] * sh[(d - 1) * L:d * L, :]
    return _silu(acc)


def _decay_terms(dtt_ref, dtb_ref, alog_ref, tril3_ref):
    L = SSD_L
    raw_t = jnp.concatenate([dtt_ref[:, c * L:(c + 1) * L] for c in range(SSD_CHUNKS)], axis=0)
    u = raw_t.T + dtb_ref[...]
    dt = jnp.maximum(u, 0.0) + jnp.log1p(jnp.exp(-jnp.abs(u)))
    a = dt * (-jnp.exp(alog_ref[...]) * LOG2E)
    a_parts = jnp.concatenate(_split3(a), axis=0).astype(BF16)
    a_cum = jnp.dot(tril3_ref[...], a_parts, preferred_element_type=F32)
    a_last = a_cum[L - 1:L, :]
    w = dt * jnp.exp2(a_last - a_cum)
    cd = jnp.broadcast_to(jnp.exp2(a_last), (SUBLANES, LANES))
    stack = jnp.concatenate([w, jnp.exp2(a_cum), cd], axis=0)
    packed = jnp.concatenate(_split3(stack), axis=1).astype(BF16)
    return dict(a_cum=a_cum, b_t=(a_cum - jnp.log2(dt)).T, packed=packed)


def _decay_weights(c, terms, e3c_ref):
    L = SSD_L
    ex = jnp.dot(terms["packed"], e3c_ref[c], preferred_element_type=F32)
    return ex[:L, :], ex[L:2 * L, :], ex[2 * L:2 * L + 1, :]


def _ssd_front(tiles, w_exp, e_exp, cd):
    xc = jnp.concatenate(tiles, axis=1)
    xs = xc[:, :SSD_WIDTH]
    xs_bf = xs.astype(BF16)
    bm = xc[:, SSD_WIDTH:SSD_WIDTH + SSD_GROUPS * SSD_STATE]
    cm = xc[:, SSD_WIDTH + SSD_GROUPS * SSD_STATE:]
    xd = (xs * w_exp).astype(BF16)
    return xs, xs_bf, bm, cm, xd, cd, e_exp


def _ssd_back(c, front, terms, gz_ref, dexp_ref, y_ref, state, causal, low, hooks):
    L = SSD_L
    gw = SSD_WIDTH // SSD_GROUPS
    ppg = SSD_HEADS // SSD_GROUPS // 2
    rows = slice(c * L, (c + 1) * L)
    xs, xs_bf, bm, cm, xd, cd, e_exp = front
    a_cum, b_t = terms["a_cum"], terms["b_t"]
    lane0 = c * SSD_HEADS

    def finish(p, lhs, rhs, zp):
        ch = slice(p * PAIR, (p + 1) * PAIR)
        yp = jnp.dot(lhs, rhs, preferred_element_type=F32) + e_exp[:, ch] * zp
        yp = yp + dexp_ref[:, ch] * xs[:, ch]
        gz = gz_ref[0, rows, p * PAIR:(p + 1) * PAIR].astype(F32)
        y_ref[0, rows, p * PAIR:(p + 1) * PAIR] = (yp * gz).astype(BF16)

    zero = jnp.zeros((L, PAIR), BF16)
    pending = None
    for g in range(SSD_GROUPS):
        bg = bm[:, g * SSD_STATE:(g + 1) * SSD_STATE]
        cg = cm[:, g * SSD_STATE:(g + 1) * SSD_STATE]
        cb = lax.dot_general(cg.astype(BF16), bg.astype(BF16), (((1,), (1,)), ((), ())),
                             preferred_element_type=F32)
        sg = state[g]
        zg = jnp.dot(cg.astype(BF16), sg.astype(BF16), preferred_element_type=F32)
        for q in range(ppg):
            p = g * ppg + q
            lhs = []
            for h in (lane0 + 2 * p, lane0 + 2 * p + 1):
                seg = a_cum[:, h:h + 1] - b_t[h:h + 1, :]
                lhs.append((cb * jnp.where(causal, jnp.exp2(seg), 0.0)).astype(BF16))
            xp = xs_bf[:, p * PAIR:(p + 1) * PAIR]
            rhs = jnp.concatenate([jnp.where(low, xp, zero), jnp.where(low, zero, xp)], axis=0)
            lhs = jnp.concatenate(lhs, axis=1)
            if pending is not None:
                finish(*pending)
            pending = (p, lhs, rhs, zg[:, q * PAIR:(q + 1) * PAIR])
            if p < len(hooks):
                hooks[p]()
        finish(*pending)
        pending = None
        s_new = jnp.dot(bg.T.astype(BF16), xd[:, g * gw:(g + 1) * gw],
                        preferred_element_type=F32)
        state[g] = sg * cd[:, g * gw:(g + 1) * gw] + s_new


def _ssd_kernel(gz_ref, xbc_ref, dtt_ref, cw_ref, cb_ref, dtb_ref, alog_ref,
                dexp_ref, e3c_ref, tril3_ref, shift_ref, y_ref, ubuf, state):
    j = pl.program_id(1)
    T = SSD_T
    L = SSD_L

    @pl.when(j == 0)
    def _():
        ubuf[0:L, :] = jnp.zeros((L, CONV_DIM), BF16)
        state[...] = jnp.zeros(state.shape, F32)

    @pl.when(j > 0)
    def _():
        ubuf[0:L, :] = ubuf[T:T + L, :]

    ubuf[L:T + L, :] = xbc_ref[0]

    row = lax.broadcasted_iota(jnp.int32, (L, L), 0)
    col = lax.broadcasted_iota(jnp.int32, (L, L), 1)
    causal = row >= col
    low = lax.broadcasted_iota(jnp.int32, (L, PAIR), 1) < SSD_HEAD_DIM
    def conv(c, k):
        return _conv_tile(c, k, cw_ref, cb_ref, shift_ref, ubuf)

    terms = _decay_terms(dtt_ref, dtb_ref, alog_ref, tril3_ref)
    n = SSD_CHUNKS
    cur = _ssd_front([conv(0, k) for k in range(N_CONV_TILES)],
                     *_decay_weights(0, terms, e3c_ref))
    for c in range(n):
        nxt = {}
        hooks = []
        if c + 1 < n:
            def h_conv(k, cc=c + 1):
                nxt.setdefault("tiles", []).append(conv(cc, k))

            def h_weights(cc=c + 1):
                nxt["w"] = _decay_weights(cc, terms, e3c_ref)

            hooks = [functools.partial(h_conv, k) for k in range(N_CONV_TILES)] + [h_weights]
        _ssd_back(c, cur, terms, gz_ref, dexp_ref, y_ref, state, causal, low, hooks)
        if c + 1 < n:
            cur = _ssd_front(nxt["tiles"], *nxt["w"])


def _mixers(q, kt, v, table, gz, xbc, dtt_raw, conv_w, conv_b, dt_bias_p, a_log_p, d_exp,
            e3c, tril3, shift):
    b, s, _ = gz.shape
    T = SSD_T
    assert T == ATT_NB * ATT_TQ
    nblk = s // T
    blk = lambda bi, j: (bi, j, 0)
    colblk = lambda bi, j: (0, bi * nblk + j)
    const = lambda bi, j: (0, 0)
    once = pl.Buffered(1)
    params = pltpu.CompilerParams(dimension_semantics=("parallel", "arbitrary"),
                                  vmem_limit_bytes=VMEM_LIMIT)
    att = pl.pallas_call(
        _attn_kernel,
        out_shape=jax.ShapeDtypeStruct((b, s, ATT_WIDTH), BF16),
        grid=(b, nblk),
        in_specs=[
            pl.BlockSpec((1, T, ATT_WIDTH), blk),
            pl.BlockSpec((ATT_WIDTH, T), colblk),
            pl.BlockSpec((1, T, ATT_WIDTH), blk),
            pl.BlockSpec((ATT_HEADS + 1, ATT_TQ, ATT_BAND), lambda bi, j: (0, 0, 0),
                         pipeline_mode=once),
        ],
        out_specs=pl.BlockSpec((1, T, ATT_WIDTH), blk),
        scratch_shapes=[
            pltpu.VMEM((ATT_RING, ATT_HEADS * PAIR, ATT_TQ), BF16),
            pltpu.VMEM((ATT_RING, ATT_TQ, ATT_HEADS * PAIR), BF16),
        ],
        compiler_params=params,
        name="band_attn",
    )(q, kt, v, table)
    ssd = pl.pallas_call(
        _ssd_kernel,
        out_shape=jax.ShapeDtypeStruct((b, s, SSD_WIDTH), BF16),
        grid=(b, nblk),
        in_specs=[
            pl.BlockSpec((1, T, SSD_WIDTH), blk),
            pl.BlockSpec((1, T, CONV_DIM), blk),
            pl.BlockSpec((SSD_HEADS, T), colblk),
            pl.BlockSpec((SSD_CONV, CONV_DIM), const),
            pl.BlockSpec((1, CONV_DIM), const),
            pl.BlockSpec((1, LANES), const),
            pl.BlockSpec((1, LANES), const),
            pl.BlockSpec((1, SSD_WIDTH), const),
            pl.BlockSpec((SSD_CHUNKS, 3 * LANES, SSD_WIDTH), lambda bi, j: (0, 0, 0),
                         pipeline_mode=once),
            pl.BlockSpec((SSD_L, 3 * SSD_L), const, pipeline_mode=once),
            pl.BlockSpec(((SSD_CONV - 1) * SSD_L, 2 * SSD_L), const, pipeline_mode=once),
        ],
        out_specs=pl.BlockSpec((1, T, SSD_WIDTH), blk),
        scratch_shapes=[
            pltpu.VMEM((T + SSD_L, CONV_DIM), BF16),
            pltpu.VMEM((SSD_GROUPS, SSD_STATE, SSD_WIDTH // SSD_GROUPS), F32),
        ],
        compiler_params=params,
        name="ssd_scan",
    )(gz, xbc, dtt_raw, conv_w, conv_b, dt_bias_p, a_log_p, d_exp, e3c, tril3, shift)
    return att, ssd


def _shift_matrix():
    r = jnp.arange((SSD_CONV - 1) * SSD_L)[:, None]
    c = jnp.arange(2 * SSD_L)[None, :]
    return (c == SSD_L + r % SSD_L - (r // SSD_L + 1)).astype(BF16)


def _tril3_matrix():
    r = jnp.arange(SSD_L)[:, None]
    c = jnp.arange(3 * SSD_L)[None, :] % SSD_L
    return (r >= c).astype(BF16)


def _expand_matrices():
    r = jnp.arange(3 * LANES)[None, :, None] % LANES
    c = jnp.arange(SSD_CHUNKS)[:, None, None]
    col = jnp.arange(SSD_WIDTH)[None, None, :]
    hit = (r // SSD_HEADS == c) & (r % SSD_HEADS == col // SSD_HEAD_DIM)
    return hit.astype(BF16)


def _out_ffn_kernel(x_ref, att_ref, ssd_ref, sw_ref, wo_ref, nw_ref,
                    w1_ref, w2_ref, fw_ref, o_ref):
    gw = SSD_WIDTH // SSD_GROUPS
    ssd = []
    for g in range(SSD_GROUPS):
        yg = ssd_ref[:, g * gw:(g + 1) * gw].astype(F32)
        ssd.append(_rms(yg, sw_ref[:, g * gw:(g + 1) * gw]).astype(BF16))
    ssd = jnp.concatenate(ssd, axis=1)
    h = (x_ref[...]
         + jnp.dot(att_ref[...], wo_ref[:ATT_WIDTH, :], preferred_element_type=F32)
         + jnp.dot(ssd, wo_ref[ATT_WIDTH:, :], preferred_element_type=F32))
    hn = _rms(h, nw_ref[...]).astype(BF16)
    hid = jnp.dot(hn, w1_ref[...], preferred_element_type=F32)
    hid = jnp.square(jnp.maximum(hid, 0.0)).astype(BF16)
    h = h + jnp.dot(hid, w2_ref[...], preferred_element_type=F32)
    o_ref[...] = _rms(h, fw_ref[...])


def _out_ffn(x2, att2, ssd2, ssd_norm_w, w_out, norm_w, w1, w2, final_w):
    m = x2.shape[0]
    tm = FFN_TM
    row = lambda i: (i, 0)
    const = lambda i: (0, 0)
    once = pl.Buffered(1)
    return pl.pallas_call(
        _out_ffn_kernel,
        out_shape=jax.ShapeDtypeStruct((m, D_MODEL), F32),
        grid=(m // tm,),
        in_specs=[
            pl.BlockSpec((tm, D_MODEL), row),
            pl.BlockSpec((tm, ATT_WIDTH), row),
            pl.BlockSpec((tm, SSD_WIDTH), row),
            pl.BlockSpec((1, SSD_WIDTH), const),
            pl.BlockSpec((ATT_WIDTH + SSD_WIDTH, D_MODEL), const, pipeline_mode=once),
            pl.BlockSpec((1, D_MODEL), const),
            pl.BlockSpec((D_MODEL, D_FF), const, pipeline_mode=once),
            pl.BlockSpec((D_FF, D_MODEL), const, pipeline_mode=once),
            pl.BlockSpec((1, D_MODEL), const),
        ],
        out_specs=pl.BlockSpec((tm, D_MODEL), row),
        compiler_params=pltpu.CompilerParams(
            dimension_semantics=("parallel",), vmem_limit_bytes=VMEM_LIMIT),
        name="out_ffn",
    )(x2, att2, ssd2, ssd_norm_w, w_out, norm_w, w1, w2, final_w)


def _layer(x, norm_mix_w, w_in, rel_bias, conv_w, conv_b, dt_bias, a_log, d_skip,
           ssd_norm_w, w_out, norm_mlp_w, w_ff1, w_ff2, out_norm_w):
    b, s, d = x.shape
    m = b * s
    x2 = x.reshape(m, d)

    scale = ATT_HEAD_DIM ** -0.5 * LOG2E
    col = jnp.arange(MAIN_PROJ + SSD_HEADS)[None, :]
    w_main = (w_in * jnp.where(col < ATT_WIDTH, scale, 1.0)).astype(BF16)
    w_ktd = jnp.concatenate([w_in[:, ATT_WIDTH:2 * ATT_WIDTH], w_in[:, MAIN_PROJ:]],
                            axis=1).T.astype(BF16)
    per_lane = lambda v: jnp.tile(v.astype(F32), SSD_CHUNKS).reshape(1, LANES)

    q, kt, v, gz, xbc, dtt_raw = _in_proj(x2, norm_mix_w.reshape(1, d), w_main, w_ktd)

    att, ssd = _mixers(q.reshape(b, s, -1), kt, v.reshape(b, s, -1), _bias_table(rel_bias),
                       gz.reshape(b, s, -1), xbc.reshape(b, s, -1), dtt_raw,
                       conv_w.astype(F32), conv_b.reshape(1, -1).astype(F32),
                       per_lane(dt_bias), per_lane(a_log),
                       jnp.repeat(d_skip.astype(F32), SSD_HEAD_DIM).reshape(1, -1),
                       _expand_matrices(), _tril3_matrix(), _shift_matrix())

    y = _out_ffn(x2, att.reshape(m, -1), ssd.reshape(m, -1),
                 ssd_norm_w.reshape(1, -1).astype(F32),
                 w_out.astype(BF16),
                 norm_mlp_w.reshape(1, d), w_ff1.astype(BF16), w_ff2.astype(BF16),
                 out_norm_w.reshape(1, d))
    return y.reshape(b, s, d)


def kernel(x, norm_mix_w, w_in, rel_bias, conv_w, conv_b, dt_bias, a_log, d_skip,
           ssd_norm_w, w_out, norm_mlp_w, w_ff1, w_ff2, norm_final_w):
    depth = w_in.shape[0]
    assert depth == 1, "final norm is fused into the single layer's last kernel"
    return _layer(x, norm_mix_w[0], w_in[0], rel_bias[0], conv_w[0], conv_b[0],
                  dt_bias[0], a_log[0], d_skip[0], ssd_norm_w[0], w_out[0],
                  norm_mlp_w[0], w_ff1[0], w_ff2[0], norm_final_w)
```

```python
import functools

import jax
import jax.numpy as jnp
from jax import lax
from jax.experimental import pallas as pl
from jax.experimental.pallas import tpu as pltpu

F32 = jnp.float32
BF16 = jnp.bfloat16

D_MODEL = 1024
CHUNK = 64
ATT_HEADS = 8
ATT_HEAD_DIM = 64
ATT_WIDTH = ATT_HEADS * ATT_HEAD_DIM
N_PREV_CHUNKS = 8
REL_CLIP = 256
SSD_HEAD_DIM = 64
SSD_WIDTH = D_MODEL
SSD_HEADS = SSD_WIDTH // SSD_HEAD_DIM
SSD_GROUPS = 2
SSD_STATE = 128
SSD_CONV = 4
CONV_DIM = SSD_WIDTH + 2 * SSD_GROUPS * SSD_STATE
MAIN_PROJ = 3 * ATT_WIDTH + SSD_WIDTH + CONV_DIM
D_FF = 4 * D_MODEL
EPS = 1e-5
NEG = -1e30

LANES = 128
SUBLANES = 8

PROJ_TM = 1024
ATT_TQ = 256
ATT_BAND = ATT_TQ + N_PREV_CHUNKS * CHUNK
SSD_L = 128
SSD_T = 1024
SSD_CHUNKS = SSD_T // SSD_L
assert SSD_CHUNKS * SSD_HEADS == LANES
EXPAND_TERMS = 2
FFN_TM = 1024
VMEM_LIMIT = 56 * 1024 * 1024


def _rms(x, w):
    ms = jnp.mean(x * x, axis=-1, keepdims=True)
    return x * lax.rsqrt(ms + EPS) * w


def _silu(x):
    h = 0.5 * x
    return h + h * jnp.tanh(h)


def _in_proj_kernel(x_ref, nw_ref, wm_ref, wkt_ref,
                    q_ref, kt_ref, v_ref, gz_ref, xbc_ref, dtt_ref):
    hn = _rms(x_ref[...], nw_ref[...]).astype(BF16)

    def seg(lo, hi):
        return jnp.dot(hn, wm_ref[:, lo:hi], preferred_element_type=F32).astype(BF16)

    a = ATT_WIDTH
    zlo = 3 * a
    xlo = zlo + SSD_WIDTH
    q_ref[...] = seg(0, a)
    ktd = lax.dot_general(wkt_ref[...], hn, (((1,), (1,)), ((), ())),
                          preferred_element_type=F32)
    kt_ref[...] = ktd[:a, :].astype(BF16)
    dtt_ref[...] = ktd[a:, :]
    v_ref[...] = seg(2 * a, 3 * a)
    gz_ref[...] = _silu(jnp.dot(hn, wm_ref[:, zlo:xlo],
                                preferred_element_type=F32)).astype(BF16)
    xbc_ref[...] = seg(xlo, MAIN_PROJ)


def _in_proj(x2, norm_w, w_main, w_ktd):
    m = x2.shape[0]
    tm = PROJ_TM
    row = lambda i: (i, 0)
    col = lambda i: (0, i)
    const = lambda i: (0, 0)
    once = pl.Buffered(1)
    out_shape = (
        jax.ShapeDtypeStruct((m, ATT_WIDTH), BF16),
        jax.ShapeDtypeStruct((ATT_WIDTH, m), BF16),
        jax.ShapeDtypeStruct((m, ATT_WIDTH), BF16),
        jax.ShapeDtypeStruct((m, SSD_WIDTH), BF16),
        jax.ShapeDtypeStruct((m, CONV_DIM), BF16),
        jax.ShapeDtypeStruct((SSD_HEADS, m), F32),
    )
    return pl.pallas_call(
        _in_proj_kernel,
        out_shape=out_shape,
        grid=(m // tm,),
        in_specs=[
            pl.BlockSpec((tm, D_MODEL), row),
            pl.BlockSpec((1, D_MODEL), const),
            pl.BlockSpec((D_MODEL, MAIN_PROJ + SSD_HEADS), const, pipeline_mode=once),
            pl.BlockSpec((ATT_WIDTH + SSD_HEADS, D_MODEL), const, pipeline_mode=once),
        ],
        out_specs=(
            pl.BlockSpec((tm, ATT_WIDTH), row),
            pl.BlockSpec((ATT_WIDTH, tm), col),
            pl.BlockSpec((tm, ATT_WIDTH), row),
            pl.BlockSpec((tm, SSD_WIDTH), row),
            pl.BlockSpec((tm, CONV_DIM), row),
            pl.BlockSpec((SSD_HEADS, tm), col),
        ),
        compiler_params=pltpu.CompilerParams(
            dimension_semantics=("parallel",), vmem_limit_bytes=VMEM_LIMIT),
        name="in_proj",
    )(x2, norm_w, w_main, w_ktd)


ATT_SLOTS = ATT_BAND // ATT_TQ
ATT_NB = 4
ATT_RING = ATT_NB + ATT_SLOTS - 1
ATT_AHEAD = 2
PAIR = 2 * ATT_HEAD_DIM
LOG2E = 1.4426950408889634


def _bias_table_kernel(r_ref, o_ref):
    x = jnp.broadcast_to(r_ref[0], (ATT_TQ, ATT_BAND + ATT_TQ))
    rolled = pltpu.roll(x, 0, 1, stride=1, stride_axis=0)
    t = rolled[:, ATT_TQ:]
    qc = lax.broadcasted_iota(jnp.int32, (ATT_TQ, ATT_BAND), 0) // CHUNK
    kc = lax.broadcasted_iota(jnp.int32, (ATT_TQ, ATT_BAND), 1) // CHUNK
    valid = (kc >= qc) & (kc <= qc + N_PREV_CHUNKS)
    valid = valid & (pl.program_id(0) < ATT_HEADS)
    o_ref[0] = jnp.where(valid, t * LOG2E, NEG)


def _bias_table(rel_bias):
    rb = rel_bias.astype(F32).T
    n = ATT_BAND + ATT_TQ
    far = jnp.broadcast_to(rb[:, 2 * REL_CLIP:], (ATT_HEADS, n - 2 * REL_CLIP + 1))
    near = rb[:, 2 * REL_CLIP - 1:0:-1]
    r = jnp.concatenate([far, near], axis=1)
    r = jnp.concatenate([r, jnp.zeros((1, n), F32)], axis=0).reshape(ATT_HEADS + 1, 1, n)
    return pl.pallas_call(
        _bias_table_kernel,
        out_shape=jax.ShapeDtypeStruct((ATT_HEADS + 1, ATT_TQ, ATT_BAND), F32),
        grid=(ATT_HEADS + 1,),
        in_specs=[pl.BlockSpec((1, 1, n), lambda h: (h, 0, 0))],
        out_specs=pl.BlockSpec((1, ATT_TQ, ATT_BAND), lambda h: (h, 0, 0)),
        compiler_params=pltpu.CompilerParams(dimension_semantics=("parallel",)),
        name="bias_table",
    )(r)


def _attn_kernel(q_ref, kt_ref, v_ref, tab_ref, o_ref, kring, vring):
    i = pl.program_id(1)
    tq = ATT_TQ
    hd = ATT_HEAD_DIM
    lane = lax.broadcasted_iota(jnp.int32, (tq, PAIR), 1)
    low = lane < hd

    @pl.when(i == 0)
    def _():
        kring[...] = jnp.zeros(kring.shape, BF16)
        vring[...] = jnp.zeros(vring.shape, BF16)

    for j in range(ATT_NB):
        slot_new = lax.rem(ATT_NB * i + j, ATT_RING)
        cols = slice(j * tq, (j + 1) * tq)
        for h in range(ATT_HEADS):
            r0 = h * PAIR + (h % 2) * hd
            kring[slot_new, r0:r0 + hd, :] = kt_ref[h * hd:(h + 1) * hd, cols]
        for p in range(ATT_HEADS // 2):
            vp = v_ref[0, cols, p * PAIR:(p + 1) * PAIR]
            one = jnp.ones_like(vp)
            vring[slot_new, :, (2 * p) * PAIR:(2 * p + 1) * PAIR] = jnp.where(low, vp, one)
            vring[slot_new, :, (2 * p + 1) * PAIR:(2 * p + 2) * PAIR] = jnp.where(low, one, vp)

    def band(j):
        n = [ATT_NB * i + j - (ATT_SLOTS - 1) + m for m in range(ATT_SLOTS)]
        slots = [lax.rem(nm + ATT_RING, ATT_RING) for nm in n]
        masked = [nm < 0 for nm in n]
        return slots, masked

    bands = [band(j) for j in range(ATT_NB)]

    def scores(j, h):
        slots, masked = bands[j]
        p = h // 2
        qp = q_ref[0, j * tq:(j + 1) * tq, p * PAIR:(p + 1) * PAIR]
        return [jnp.dot(qp, kring[slots[m], h * PAIR:(h + 1) * PAIR, :],
                        preferred_element_type=F32)
                + tab_ref[jnp.where(masked[m], ATT_HEADS, h), :, m * tq:(m + 1) * tq]
                for m in range(ATT_SLOTS)]

    items = [(j, h) for j in range(ATT_NB) for h in range(ATT_HEADS)]
    queue = [scores(*it) for it in items[:ATT_AHEAD]]
    res = []
    for idx, (j, h) in enumerate(items):
        s = queue.pop(0)
        if idx + ATT_AHEAD < len(items):
            queue.append(scores(*items[idx + ATT_AHEAD]))
        slots, _ = bands[j]
        mx = functools.reduce(jnp.maximum, s)
        mx = jnp.max(mx, axis=-1, keepdims=True)
        acc = None
        for m in range(ATT_SLOTS):
            e = jnp.exp2(s[m] - mx).astype(BF16)
            pv = jnp.dot(e, vring[slots[m], :, h * PAIR:(h + 1) * PAIR],
                         preferred_element_type=F32)
            acc = pv if acc is None else acc + pv
        res.append(acc)
        if h % 2 == 1:
            p = h // 2
            num = jnp.where(low, res[0], res[1])
            den = jnp.where(low, pltpu.roll(res[0], hd, 1), pltpu.roll(res[1], hd, 1))
            o_ref[0, j * tq:(j + 1) * tq, p * PAIR:(p + 1) * PAIR] = (num / den).astype(BF16)
            res = []


def _split3(v):
    v1 = v.astype(BF16).astype(F32)
    r1 = v - v1
    v2 = r1.astype(BF16).astype(F32)
    v3 = (r1 - v2).astype(BF16).astype(F32)
    return v1, v2, v3


CONV_TILE = 256
N_CONV_TILES = CONV_DIM // CONV_TILE


def _conv_tile(c, k, cw_ref, cb_ref, shift_ref, ubuf):
    L = SSD_L
    ch = slice(k * CONV_TILE, (k + 1) * CONV_TILE)
    win = ubuf[c * L:c * L + 2 * L, ch]
    sh = jnp.dot(shift_ref[...], win, preferred_element_type=F32)
    acc = cb_ref[:, ch] + cw_ref[SSD_CONV - 1:SSD_CONV, ch] * win[L:, :].astype(F32)
    for d in range(1, SSD_CONV):
        acc = acc + cw_ref[SSD_CONV - 1 - d:SSD_CONV - d, ch] * sh[(d - 1) * L:d * L, :]
    return _silu(acc)


def _decay_terms(dtt_ref, dtb_ref, alog_ref, tril3_ref):
    L = SSD_L
    raw_t = jnp.concatenate([dtt_ref[:, c * L:(c + 1) * L] for c in range(SSD_CHUNKS)], axis=0)
    u = raw_t.T + dtb_ref[...]
    dt = jnp.maximum(u, 0.0) + jnp.log1p(jnp.exp(-jnp.abs(u)))
    a = dt * (-jnp.exp(alog_ref[...]) * LOG2E)
    a_parts = jnp.concatenate(_split3(a), axis=0).astype(BF16)
    a_cum = jnp.dot(tril3_ref[...], a_parts, preferred_element_type=F32)
    a_last = a_cum[L - 1:L, :]
    w = dt * jnp.exp2(a_last - a_cum)
    cd = jnp.broadcast_to(jnp.exp2(a_last), (SUBLANES, LANES))
    stack = jnp.concatenate([w, jnp.exp2(a_cum), cd], axis=0)
    packed = jnp.concatenate(_split3(stack)[:EXPAND_TERMS], axis=1).astype(BF16)
    return dict(a_cum=a_cum, b_t=(a_cum - jnp.log2(dt)).T, packed=packed)


def _decay_weights(c, terms, e3c_ref):
    L = SSD_L
    ex = jnp.dot(terms["packed"], e3c_ref[c], preferred_element_type=F32)
    return ex[:L, :], ex[L:2 * L, :], ex[2 * L:2 * L + 1, :]


def _ssd_front(tiles, w_exp, e_exp, cd):
    xc = jnp.concatenate(tiles, axis=1)
    xs = xc[:, :SSD_WIDTH]
    xs_bf = xs.astype(BF16)
    bm = xc[:, SSD_WIDTH:SSD_WIDTH + SSD_GROUPS * SSD_STATE]
    cm = xc[:, SSD_WIDTH + SSD_GROUPS * SSD_STATE:]
    xd = (xs * w_exp).astype(BF16)
    return xs, xs_bf, bm, cm, xd, cd, e_exp


def _ssd_back(c, front, terms, gz_ref, dexp_ref, y_ref, state, causal, low, hooks):
    L = SSD_L
    gw = SSD_WIDTH // SSD_GROUPS
    ppg = SSD_HEADS // SSD_GROUPS // 2
    rows = slice(c * L, (c + 1) * L)
    xs, xs_bf, bm, cm, xd, cd, e_exp = front
    a_cum, b_t = terms["a_cum"], terms["b_t"]
    lane0 = c * SSD_HEADS

    def finish(p, lhs, rhs, zp):
        ch = slice(p * PAIR, (p + 1) * PAIR)
        yp = jnp.dot(lhs, rhs, preferred_element_type=F32) + e_exp[:, ch] * zp
        yp = yp + dexp_ref[:, ch] * xs[:, ch]
        gz = gz_ref[0, rows, p * PAIR:(p + 1) * PAIR].astype(F32)
        y_ref[0, rows, p * PAIR:(p + 1) * PAIR] = (yp * gz).astype(BF16)

    zero = jnp.zeros((L, PAIR), BF16)
    pending = None
    for g in range(SSD_GROUPS):
        bg = bm[:, g * SSD_STATE:(g + 1) * SSD_STATE]
        cg = cm[:, g * SSD_STATE:(g + 1) * SSD_STATE]
        cb = lax.dot_general(cg.astype(BF16), bg.astype(BF16), (((1,), (1,)), ((), ())),
                             preferred_element_type=F32)
        sg = state[g]
        zg = jnp.dot(cg.astype(BF16), sg.astype(BF16), preferred_element_type=F32)
        for q in range(ppg):
            p = g * ppg + q
            lhs = []
            for h in (lane0 + 2 * p, lane0 + 2 * p + 1):
                seg = a_cum[:, h:h + 1] - b_t[h:h + 1, :]
                lhs.append((cb * jnp.where(causal, jnp.exp2(seg), 0.0)).astype(BF16))
            xp = xs_bf[:, p * PAIR:(p + 1) * PAIR]
            rhs = jnp.concatenate([jnp.where(low, xp, zero), jnp.where(low, zero, xp)], axis=0)
            lhs = jnp.concatenate(lhs, axis=1)
            if pending is not None:
                finish(*pending)
            pending = (p, lhs, rhs, zg[:, q * PAIR:(q + 1) * PAIR])
            if p < len(hooks):
                hooks[p]()
        finish(*pending)
        pending = None
        s_new = jnp.dot(bg.T.astype(BF16), xd[:, g * gw:(g + 1) * gw],
                        preferred_element_type=F32)
        state[g] = sg * cd[:, g * gw:(g + 1) * gw] + s_new


def _ssd_kernel(gz_ref, xbc_ref, dtt_ref, cw_ref, cb_ref, dtb_ref, alog_ref,
                dexp_ref, e3c_ref, tril3_ref, shift_ref, y_ref, ubuf, state):
    j = pl.program_id(1)
    T = SSD_T
    L = SSD_L

    @pl.when(j == 0)
    def _():
        ubuf[0:L, :] = jnp.zeros((L, CONV_DIM), BF16)
        state[...] = jnp.zeros(state.shape, F32)

    @pl.when(j > 0)
    def _():
        ubuf[0:L, :] = ubuf[T:T + L, :]

    ubuf[L:T + L, :] = xbc_ref[0]

    row = lax.broadcasted_iota(jnp.int32, (L, L), 0)
    col = lax.broadcasted_iota(jnp.int32, (L, L), 1)
    causal = row >= col
    low = lax.broadcasted_iota(jnp.int32, (L, PAIR), 1) < SSD_HEAD_DIM
    def conv(c, k):
        return _conv_tile(c, k, cw_ref, cb_ref, shift_ref, ubuf)

    terms = _decay_terms(dtt_ref, dtb_ref, alog_ref, tril3_ref)
    n = SSD_CHUNKS
    cur = _ssd_front([conv(0, k) for k in range(N_CONV_TILES)],
                     *_decay_weights(0, terms, e3c_ref))
    for c in range(n):
        nxt = {}
        hooks = []
        if c + 1 < n:
            def h_conv(k, cc=c + 1):
                nxt.setdefault("tiles", []).append(conv(cc, k))

            def h_weights(cc=c + 1):
                nxt["w"] = _decay_weights(cc, terms, e3c_ref)

            hooks = [functools.partial(h_conv, k) for k in range(N_CONV_TILES)] + [h_weights]
        _ssd_back(c, cur, terms, gz_ref, dexp_ref, y_ref, state, causal, low, hooks)
        if c + 1 < n:
            cur = _ssd_front(nxt["tiles"], *nxt["w"])


def _mixers(q, kt, v, table, gz, xbc, dtt_raw, conv_w, conv_b, dt_bias_p, a_log_p, d_exp,
            e3c, tril3, shift):
    b, s, _ = gz.shape
    T = SSD_T
    assert T == ATT_NB * ATT_TQ
    nblk = s // T
    blk = lambda bi, j: (bi, j, 0)
    colblk = lambda bi, j: (0, bi * nblk + j)
    const = lambda bi, j: (0, 0)
    once = pl.Buffered(1)
    params = pltpu.CompilerParams(dimension_semantics=("parallel", "arbitrary"),
                                  vmem_limit_bytes=VMEM_LIMIT)
    att = pl.pallas_call(
        _attn_kernel,
        out_shape=jax.ShapeDtypeStruct((b, s, ATT_WIDTH), BF16),
        grid=(b, nblk),
        in_specs=[
            pl.BlockSpec((1, T, ATT_WIDTH), blk),
            pl.BlockSpec((ATT_WIDTH, T), colblk),
            pl.BlockSpec((1, T, ATT_WIDTH), blk),
            pl.BlockSpec((ATT_HEADS + 1, ATT_TQ, ATT_BAND), lambda bi, j: (0, 0, 0),
                         pipeline_mode=once),
        ],
        out_specs=pl.BlockSpec((1, T, ATT_WIDTH), blk),
        scratch_shapes=[
            pltpu.VMEM((ATT_RING, ATT_HEADS * PAIR, ATT_TQ), BF16),
            pltpu.VMEM((ATT_RING, ATT_TQ, ATT_HEADS * PAIR), BF16),
        ],
        compiler_params=params,
        name="band_attn",
    )(q, kt, v, table)
    ssd = pl.pallas_call(
        _ssd_kernel,
        out_shape=jax.ShapeDtypeStruct((b, s, SSD_WIDTH), BF16),
        grid=(b, nblk),
        in_specs=[
            pl.BlockSpec((1, T, SSD_WIDTH), blk),
            pl.BlockSpec((1, T, CONV_DIM), blk),
            pl.BlockSpec((SSD_HEADS, T), colblk),
            pl.BlockSpec((SSD_CONV, CONV_DIM), const),
            pl.BlockSpec((1, CONV_DIM), const),
            pl.BlockSpec((1, LANES), const),
            pl.BlockSpec((1, LANES), const),
            pl.BlockSpec((1, SSD_WIDTH), const),
            pl.BlockSpec((SSD_CHUNKS, EXPAND_TERMS * LANES, SSD_WIDTH), lambda bi, j: (0, 0, 0),
                         pipeline_mode=once),
            pl.BlockSpec((SSD_L, 3 * SSD_L), const, pipeline_mode=once),
            pl.BlockSpec(((SSD_CONV - 1) * SSD_L, 2 * SSD_L), const, pipeline_mode=once),
        ],
        out_specs=pl.BlockSpec((1, T, SSD_WIDTH), blk),
        scratch_shapes=[
            pltpu.VMEM((T + SSD_L, CONV_DIM), BF16),
            pltpu.VMEM((SSD_GROUPS, SSD_STATE, SSD_WIDTH // SSD_GROUPS), F32),
        ],
        compiler_params=params,
        name="ssd_scan",
    )(gz, xbc, dtt_raw, conv_w, conv_b, dt_bias_p, a_log_p, d_exp, e3c, tril3, shift)
    return att, ssd


def _shift_matrix():
    r = jnp.arange((SSD_CONV - 1) * SSD_L)[:, None]
    c = jnp.arange(2 * SSD_L)[None, :]
    return (c == SSD_L + r % SSD_L - (r // SSD_L + 1)).astype(BF16)


def _tril3_matrix():
    r = jnp.arange(SSD_L)[:, None]
    c = jnp.arange(3 * SSD_L)[None, :] % SSD_L
    return (r >= c).astype(BF16)


def _expand_matrices():
    r = jnp.arange(EXPAND_TERMS * LANES)[None, :, None] % LANES
    c = jnp.arange(SSD_CHUNKS)[:, None, None]
    col = jnp.arange(SSD_WIDTH)[None, None, :]
    hit = (r // SSD_HEADS == c) & (r % SSD_HEADS == col // SSD_HEAD_DIM)
    return hit.astype(BF16)


def _out_ffn_kernel(x_ref, att_ref, ssd_ref, sw_ref, wo_ref, nw_ref,
                    w1_ref, w2_ref, fw_ref, o_ref):
    gw = SSD_WIDTH // SSD_GROUPS
    ssd = []
    for g in range(SSD_GROUPS):
        yg = ssd_ref[:, g * gw:(g + 1) * gw].astype(F32)
        ssd.append(_rms(yg, sw_ref[:, g * gw:(g + 1) * gw]).astype(BF16))
    ssd = jnp.concatenate(ssd, axis=1)
    h = (x_ref[...]
         + jnp.dot(att_ref[...], wo_ref[:ATT_WIDTH, :], preferred_element_type=F32)
         + jnp.dot(ssd, wo_ref[ATT_WIDTH:, :], preferred_element_type=F32))
    hn = _rms(h, nw_ref[...]).astype(BF16)
    hid = jnp.dot(hn, w1_ref[...], preferred_element_type=F32)
    hid = jnp.square(jnp.maximum(hid, 0.0)).astype(BF16)
    h = h + jnp.dot(hid, w2_ref[...], preferred_element_type=F32)
    o_ref[...] = _rms(h, fw_ref[...])


def _out_ffn(x2, att2, ssd2, ssd_norm_w, w_out, norm_w, w1, w2, final_w):
    m = x2.shape[0]
    tm = FFN_TM
    row = lambda i: (i, 0)
    const = lambda i: (0, 0)
    once = pl.Buffered(1)
    return pl.pallas_call(
        _out_ffn_kernel,
        out_shape=jax.ShapeDtypeStruct((m, D_MODEL), F32),
        grid=(m // tm,),
        in_specs=[
            pl.BlockSpec((tm, D_MODEL), row),
            pl.BlockSpec((tm, ATT_WIDTH), row),
            pl.BlockSpec((tm, SSD_WIDTH), row),
            pl.BlockSpec((1, SSD_WIDTH), const),
            pl.BlockSpec((ATT_WIDTH + SSD_WIDTH, D_MODEL), const, pipeline_mode=once),
            pl.BlockSpec((1, D_MODEL), const),
            pl.BlockSpec((D_MODEL, D_FF), const, pipeline_mode=once),
            pl.BlockSpec((D_FF, D_MODEL), const, pipeline_mode=once),
            pl.BlockSpec((1, D_MODEL), const),
        ],
        out_specs=pl.BlockSpec((tm, D_MODEL), row),
        compiler_params=pltpu.CompilerParams(
            dimension_semantics=("parallel",), vmem_limit_bytes=VMEM_LIMIT),
        name="out_ffn",
    )(x2, att2, ssd2, ssd_norm_w, w_out, norm_w, w1, w2, final_w)


def _layer(x, norm_mix_w, w_in, rel_bias, conv_w, conv_b, dt_bias, a_log, d_skip,
           ssd_norm_w, w_out, norm_mlp_w, w_ff1, w_ff2, out_norm_w):
    b, s, d = x.shape
    m = b * s
    x2 = x.reshape(m, d)

    scale = ATT_HEAD_DIM ** -0.5 * LOG2E
    col = jnp.arange(MAIN_PROJ + SSD_HEADS)[None, :]
    w_main = (w_in * jnp.where(col < ATT_WIDTH, scale, 1.0)).astype(BF16)
    w_ktd = jnp.concatenate([w_in[:, ATT_WIDTH:2 * ATT_WIDTH], w_in[:, MAIN_PROJ:]],
                            axis=1).T.astype(BF16)
    per_lane = lambda v: jnp.tile(v.astype(F32), SSD_CHUNKS).reshape(1, LANES)

    q, kt, v, gz, xbc, dtt_raw = _in_proj(x2, norm_mix_w.reshape(1, d), w_main, w_ktd)

    att, ssd = _mixers(q.reshape(b, s, -1), kt, v.reshape(b, s, -1), _bias_table(rel_bias),
                       gz.reshape(b, s, -1), xbc.reshape(b, s, -1), dtt_raw,
                       conv_w.astype(F32), conv_b.reshape(1, -1).astype(F32),
                       per_lane(dt_bias), per_lane(a_log),
                       jnp.repeat(d_skip.astype(F32), SSD_HEAD_DIM).reshape(1, -1),
                       _expand_matrices(), _tril3_matrix(), _shift_matrix())

    y = _out_ffn(x2, att.reshape(m, -1), ssd.reshape(m, -1),
                 ssd_norm_w.reshape(1, -1).astype(F32),
                 w_out.astype(BF16),
                 norm_mlp_w.reshape(1, d), w_ff1.astype(BF16), w_ff2.astype(BF16),
                 out_norm_w.reshape(1, d))
    return y.reshape(b, s, d)


def kernel(x, norm_mix_w, w_in, rel_bias, conv_w, conv_b, dt_bias, a_log, d_skip,
           ssd_norm_w, w_out, norm_mlp_w, w_ff1, w_ff2, norm_final_w):
    depth = w_in.shape[0]
    assert depth == 1, "final norm is fused into the single layer's last kernel"
    return _layer(x, norm_mix_w[0], w_in[0], rel_bias[0], conv_w[0], conv_b[0],
                  dt_bias[0], a_log[0], d_skip[0], ssd_norm_w[0], w_out[0],
                  norm_mlp_w[0], w_ff1[0], w_ff2[0], norm_final_w)
```

```python
import functools

import jax
import jax.numpy as jnp
from jax import lax
from jax.experimental import pallas as pl
from jax.experimental.pallas import tpu as pltpu

F32 = jnp.float32
BF16 = jnp.bfloat16

D_MODEL = 1024
CHUNK = 64
ATT_HEADS = 8
ATT_HEAD_DIM = 64
ATT_WIDTH = ATT_HEADS * ATT_HEAD_DIM
N_PREV_CHUNKS = 8
REL_CLIP = 256
SSD_HEAD_DIM = 64
SSD_WIDTH = D_MODEL
SSD_HEADS = SSD_WIDTH // SSD_HEAD_DIM
SSD_GROUPS = 2
SSD_STATE = 128
SSD_CONV = 4
CONV_DIM = SSD_WIDTH + 2 * SSD_GROUPS * SSD_STATE
MAIN_PROJ = 3 * ATT_WIDTH + SSD_WIDTH + CONV_DIM
D_FF = 4 * D_MODEL
EPS = 1e-5
NEG = -1e30

LANES = 128
SUBLANES = 8

PROJ_TM = 1024
ATT_TQ = 256
ATT_BAND = ATT_TQ + N_PREV_CHUNKS * CHUNK
SSD_L = 128
SSD_T = 1024
SSD_CHUNKS = SSD_T // SSD_L
assert SSD_CHUNKS * SSD_HEADS == LANES
EXPAND_TERMS = 2
FFN_TM = 1024
FFN_SPLIT = 2
FFN_VMEM_LIMIT = 60000 * 1024
VMEM_LIMIT = 56 * 1024 * 1024


def _rms(x, w):
    ms = jnp.mean(x * x, axis=-1, keepdims=True)
    return x * lax.rsqrt(ms + EPS) * w


def _silu(x):
    h = 0.5 * x
    return h + h * jnp.tanh(h)


def _in_proj_kernel(x_ref, nw_ref, wm_ref, wkt_ref,
                    q_ref, kt_ref, v_ref, gz_ref, xbc_ref, dtt_ref):
    hn = _rms(x_ref[...], nw_ref[...]).astype(BF16)

    def seg(lo, hi):
        return jnp.dot(hn, wm_ref[:, lo:hi], preferred_element_type=F32).astype(BF16)

    a = ATT_WIDTH
    zlo = 3 * a
    xlo = zlo + SSD_WIDTH
    q_ref[...] = seg(0, a)
    ktd = lax.dot_general(wkt_ref[...], hn, (((1,), (1,)), ((), ())),
                          preferred_element_type=F32)
    kt_ref[...] = ktd[:a, :].astype(BF16)
    dtt_ref[...] = ktd[a:, :]
    v_ref[...] = seg(2 * a, 3 * a)
    gz_ref[...] = _silu(jnp.dot(hn, wm_ref[:, zlo:xlo],
                                preferred_element_type=F32)).astype(BF16)
    xbc_ref[...] = seg(xlo, MAIN_PROJ)


def _in_proj(x2, norm_w, w_main, w_ktd):
    m = x2.shape[0]
    tm = PROJ_TM
    row = lambda i: (i, 0)
    col = lambda i: (0, i)
    const = lambda i: (0, 0)
    once = pl.Buffered(1)
    out_shape = (
        jax.ShapeDtypeStruct((m, ATT_WIDTH), BF16),
        jax.ShapeDtypeStruct((ATT_WIDTH, m), BF16),
        jax.ShapeDtypeStruct((m, ATT_WIDTH), BF16),
        jax.ShapeDtypeStruct((m, SSD_WIDTH), BF16),
        jax.ShapeDtypeStruct((m, CONV_DIM), BF16),
        jax.ShapeDtypeStruct((SSD_HEADS, m), F32),
    )
    return pl.pallas_call(
        _in_proj_kernel,
        out_shape=out_shape,
        grid=(m // tm,),
        in_specs=[
            pl.BlockSpec((tm, D_MODEL), row),
            pl.BlockSpec((1, D_MODEL), const),
            pl.BlockSpec((D_MODEL, MAIN_PROJ + SSD_HEADS), const, pipeline_mode=once),
            pl.BlockSpec((ATT_WIDTH + SSD_HEADS, D_MODEL), const, pipeline_mode=once),
        ],
        out_specs=(
            pl.BlockSpec((tm, ATT_WIDTH), row),
            pl.BlockSpec((ATT_WIDTH, tm), col),
            pl.BlockSpec((tm, ATT_WIDTH), row),
            pl.BlockSpec((tm, SSD_WIDTH), row),
            pl.BlockSpec((tm, CONV_DIM), row),
            pl.BlockSpec((SSD_HEADS, tm), col),
        ),
        compiler_params=pltpu.CompilerParams(
            dimension_semantics=("parallel",), vmem_limit_bytes=VMEM_LIMIT),
        name="in_proj",
    )(x2, norm_w, w_main, w_ktd)


ATT_SLOTS = ATT_BAND // ATT_TQ
ATT_NB = 4
ATT_RING = ATT_NB + ATT_SLOTS - 1
ATT_AHEAD = 2
PAIR = 2 * ATT_HEAD_DIM
LOG2E = 1.4426950408889634


def _bias_table_kernel(r_ref, o_ref):
    x = jnp.broadcast_to(r_ref[0], (ATT_TQ, ATT_BAND + ATT_TQ))
    rolled = pltpu.roll(x, 0, 1, stride=1, stride_axis=0)
    t = rolled[:, ATT_TQ:]
    qc = lax.broadcasted_iota(jnp.int32, (ATT_TQ, ATT_BAND), 0) // CHUNK
    kc = lax.broadcasted_iota(jnp.int32, (ATT_TQ, ATT_BAND), 1) // CHUNK
    valid = (kc >= qc) & (kc <= qc + N_PREV_CHUNKS)
    valid = valid & (pl.program_id(0) < ATT_HEADS)
    o_ref[0] = jnp.where(valid, t * LOG2E, NEG)


def _bias_table(rel_bias):
    rb = rel_bias.astype(F32).T
    n = ATT_BAND + ATT_TQ
    far = jnp.broadcast_to(rb[:, 2 * REL_CLIP:], (ATT_HEADS, n - 2 * REL_CLIP + 1))
    near = rb[:, 2 * REL_CLIP - 1:0:-1]
    r = jnp.concatenate([far, near], axis=1)
    r = jnp.concatenate([r, jnp.zeros((1, n), F32)], axis=0).reshape(ATT_HEADS + 1, 1, n)
    return pl.pallas_call(
        _bias_table_kernel,
        out_shape=jax.ShapeDtypeStruct((ATT_HEADS + 1, ATT_TQ, ATT_BAND), F32),
        grid=(ATT_HEADS + 1,),
        in_specs=[pl.BlockSpec((1, 1, n), lambda h: (h, 0, 0))],
        out_specs=pl.BlockSpec((1, ATT_TQ, ATT_BAND), lambda h: (h, 0, 0)),
        compiler_params=pltpu.CompilerParams(dimension_semantics=("parallel",)),
        name="bias_table",
    )(r)


def _attn_kernel(q_ref, kt_ref, v_ref, tab_ref, o_ref, kring, vring):
    i = pl.program_id(1)
    tq = ATT_TQ
    hd = ATT_HEAD_DIM
    lane = lax.broadcasted_iota(jnp.int32, (tq, PAIR), 1)
    low = lane < hd

    @pl.when(i == 0)
    def _():
        kring[...] = jnp.zeros(kring.shape, BF16)
        vring[...] = jnp.zeros(vring.shape, BF16)

    for j in range(ATT_NB):
        slot_new = lax.rem(ATT_NB * i + j, ATT_RING)
        cols = slice(j * tq, (j + 1) * tq)
        for h in range(ATT_HEADS):
            r0 = h * PAIR + (h % 2) * hd
            kring[slot_new, r0:r0 + hd, :] = kt_ref[h * hd:(h + 1) * hd, cols]
        for p in range(ATT_HEADS // 2):
            vp = v_ref[0, cols, p * PAIR:(p + 1) * PAIR]
            one = jnp.ones_like(vp)
            vring[slot_new, :, (2 * p) * PAIR:(2 * p + 1) * PAIR] = jnp.where(low, vp, one)
            vring[slot_new, :, (2 * p + 1) * PAIR:(2 * p + 2) * PAIR] = jnp.where(low, one, vp)

    def band(j):
        n = [ATT_NB * i + j - (ATT_SLOTS - 1) + m for m in range(ATT_SLOTS)]
        slots = [lax.rem(nm + ATT_RING, ATT_RING) for nm in n]
        masked = [nm < 0 for nm in n]
        return slots, masked

    bands = [band(j) for j in range(ATT_NB)]

    def scores(j, h):
        slots, masked = bands[j]
        p = h // 2
        qp = q_ref[0, j * tq:(j + 1) * tq, p * PAIR:(p + 1) * PAIR]
        return [jnp.dot(qp, kring[slots[m], h * PAIR:(h + 1) * PAIR, :],
                        preferred_element_type=F32)
                + tab_ref[jnp.where(masked[m], ATT_HEADS, h), :, m * tq:(m + 1) * tq]
                for m in range(ATT_SLOTS)]

    items = [(j, h) for j in range(ATT_NB) for h in range(ATT_HEADS)]
    queue = [scores(*it) for it in items[:ATT_AHEAD]]
    res = []
    for idx, (j, h) in enumerate(items):
        s = queue.pop(0)
        if idx + ATT_AHEAD < len(items):
            queue.append(scores(*items[idx + ATT_AHEAD]))
        slots, _ = bands[j]
        mx = functools.reduce(jnp.maximum, s)
        mx = jnp.max(mx, axis=-1, keepdims=True)
        acc = None
        for m in range(ATT_SLOTS):
            e = jnp.exp2(s[m] - mx).astype(BF16)
            pv = jnp.dot(e, vring[slots[m], :, h * PAIR:(h + 1) * PAIR],
                         preferred_element_type=F32)
            acc = pv if acc is None else acc + pv
        res.append(acc)
        if h % 2 == 1:
            p = h // 2
            num = jnp.where(low, res[0], res[1])
            den = jnp.where(low, pltpu.roll(res[0], hd, 1), pltpu.roll(res[1], hd, 1))
            o_ref[0, j * tq:(j + 1) * tq, p * PAIR:(p + 1) * PAIR] = (num / den).astype(BF16)
            res = []


def _split3(v):
    v1 = v.astype(BF16).astype(F32)
    r1 = v - v1
    v2 = r1.astype(BF16).astype(F32)
    v3 = (r1 - v2).astype(BF16).astype(F32)
    return v1, v2, v3


CONV_TILE = 256
N_CONV_TILES = CONV_DIM // CONV_TILE


def _conv_tile(c, k, cw_ref, cb_ref, shift_ref, ubuf):
    L = SSD_L
    ch = slice(k * CONV_TILE, (k + 1) * CONV_TILE)
    win = ubuf[c * L:c * L + 2 * L, ch]
    sh = jnp.dot(shift_ref[...], win, preferred_element_type=F32)
    acc = cb_ref[:, ch] + cw_ref[SSD_CONV - 1:SSD_CONV, ch] * win[L:, :].astype(F32)
    for d in range(1, SSD_CONV):
        acc = acc + cw_ref[SSD_CONV - 1 - d:SSD_CONV - d, ch] * sh[(d - 1) * L:d * L, :]
    return _silu(acc)


def _decay_terms(dtt_ref, dtb_ref, alog_ref, tril3_ref):
    L = SSD_L
    raw_t = jnp.concatenate([dtt_ref[:, c * L:(c + 1) * L] for c in range(SSD_CHUNKS)], axis=0)
    u = raw_t.T + dtb_ref[...]
    dt = jnp.maximum(u, 0.0) + jnp.log1p(jnp.exp(-jnp.abs(u)))
    a = dt * (-jnp.exp(alog_ref[...]) * LOG2E)
    a_parts = jnp.concatenate(_split3(a), axis=0).astype(BF16)
    a_cum = jnp.dot(tril3_ref[...], a_parts, preferred_element_type=F32)
    a_last = a_cum[L - 1:L, :]
    w = dt * jnp.exp2(a_last - a_cum)
    cd = jnp.broadcast_to(jnp.exp2(a_last), (SUBLANES, LANES))
    stack = jnp.concatenate([w, jnp.exp2(a_cum), cd], axis=0)
    packed = jnp.concatenate(_split3(stack)[:EXPAND_TERMS], axis=1).astype(BF16)
    return dict(a_cum=a_cum, b_t=(a_cum - jnp.log2(dt)).T, packed=packed)


def _decay_weights(c, terms, e3c_ref):
    L = SSD_L
    ex = jnp.dot(terms["packed"], e3c_ref[c], preferred_element_type=F32)
    return ex[:L, :], ex[L:2 * L, :], ex[2 * L:2 * L + 1, :]


def _ssd_front(tiles, w_exp, e_exp, cd):
    xc = jnp.concatenate(tiles, axis=1)
    xs = xc[:, :SSD_WIDTH]
    xs_bf = xs.astype(BF16)
    bm = xc[:, SSD_WIDTH:SSD_WIDTH + SSD_GROUPS * SSD_STATE]
    cm = xc[:, SSD_WIDTH + SSD_GROUPS * SSD_STATE:]
    xd = (xs * w_exp).astype(BF16)
    return xs, xs_bf, bm, cm, xd, cd, e_exp


def _ssd_back(c, front, terms, gz_ref, dexp_ref, y_ref, state, causal, low, hooks):
    L = SSD_L
    gw = SSD_WIDTH // SSD_GROUPS
    ppg = SSD_HEADS // SSD_GROUPS // 2
    rows = slice(c * L, (c + 1) * L)
    xs, xs_bf, bm, cm, xd, cd, e_exp = front
    a_cum, b_t = terms["a_cum"], terms["b_t"]
    lane0 = c * SSD_HEADS

    def finish(p, lhs, rhs, zp):
        ch = slice(p * PAIR, (p + 1) * PAIR)
        yp = jnp.dot(lhs, rhs, preferred_element_type=F32) + e_exp[:, ch] * zp
        yp = yp + dexp_ref[:, ch] * xs[:, ch]
        gz = gz_ref[0, rows, p * PAIR:(p + 1) * PAIR].astype(F32)
        y_ref[0, rows, p * PAIR:(p + 1) * PAIR] = (yp * gz).astype(BF16)

    zero = jnp.zeros((L, PAIR), BF16)
    pending = None
    for g in range(SSD_GROUPS):
        bg = bm[:, g * SSD_STATE:(g + 1) * SSD_STATE]
        cg = cm[:, g * SSD_STATE:(g + 1) * SSD_STATE]
        cb = lax.dot_general(cg.astype(BF16), bg.astype(BF16), (((1,), (1,)), ((), ())),
                             preferred_element_type=F32)
        sg = state[g]
        zg = jnp.dot(cg.astype(BF16), sg.astype(BF16), preferred_element_type=F32)
        for q in range(ppg):
            p = g * ppg + q
            lhs = []
            for h in (lane0 + 2 * p, lane0 + 2 * p + 1):
                seg = a_cum[:, h:h + 1] - b_t[h:h + 1, :]
                lhs.append((cb * jnp.where(causal, jnp.exp2(seg), 0.0)).astype(BF16))
            xp = xs_bf[:, p * PAIR:(p + 1) * PAIR]
            rhs = jnp.concatenate([jnp.where(low, xp, zero), jnp.where(low, zero, xp)], axis=0)
            lhs = jnp.concatenate(lhs, axis=1)
            if pending is not None:
                finish(*pending)
            pending = (p, lhs, rhs, zg[:, q * PAIR:(q + 1) * PAIR])
            if p < len(hooks):
                hooks[p]()
        finish(*pending)
        pending = None
        s_new = jnp.dot(bg.T.astype(BF16), xd[:, g * gw:(g + 1) * gw],
                        preferred_element_type=F32)
        state[g] = sg * cd[:, g * gw:(g + 1) * gw] + s_new


def _ssd_kernel(gz_ref, xbc_ref, dtt_ref, cw_ref, cb_ref, dtb_ref, alog_ref,
                dexp_ref, e3c_ref, tril3_ref, shift_ref, y_ref, ubuf, state):
    j = pl.program_id(1)
    T = SSD_T
    L = SSD_L

    @pl.when(j == 0)
    def _():
        ubuf[0:L, :] = jnp.zeros((L, CONV_DIM), BF16)
        state[...] = jnp.zeros(state.shape, F32)

    @pl.when(j > 0)
    def _():
        ubuf[0:L, :] = ubuf[T:T + L, :]

    ubuf[L:T + L, :] = xbc_ref[0]

    row = lax.broadcasted_iota(jnp.int32, (L, L), 0)
    col = lax.broadcasted_iota(jnp.int32, (L, L), 1)
    causal = row >= col
    low = lax.broadcasted_iota(jnp.int32, (L, PAIR), 1) < SSD_HEAD_DIM
    def conv(c, k):
        return _conv_tile(c, k, cw_ref, cb_ref, shift_ref, ubuf)

    terms = _decay_terms(dtt_ref, dtb_ref, alog_ref, tril3_ref)
    n = SSD_CHUNKS
    cur = _ssd_front([conv(0, k) for k in range(N_CONV_TILES)],
                     *_decay_weights(0, terms, e3c_ref))
    for c in range(n):
        nxt = {}
        hooks = []
        if c + 1 < n:
            def h_conv(k, cc=c + 1):
                nxt.setdefault("tiles", []).append(conv(cc, k))

            def h_weights(cc=c + 1):
                nxt["w"] = _decay_weights(cc, terms, e3c_ref)

            hooks = [functools.partial(h_conv, k) for k in range(N_CONV_TILES)] + [h_weights]
        _ssd_back(c, cur, terms, gz_ref, dexp_ref, y_ref, state, causal, low, hooks)
        if c + 1 < n:
            cur = _ssd_front(nxt["tiles"], *nxt["w"])


def _mixers(q, kt, v, table, gz, xbc, dtt_raw, conv_w, conv_b, dt_bias_p, a_log_p, d_exp,
            e3c, tril3, shift):
    b, s, _ = gz.shape
    T = SSD_T
    assert T == ATT_NB * ATT_TQ
    nblk = s // T
    blk = lambda bi, j: (bi, j, 0)
    colblk = lambda bi, j: (0, bi * nblk + j)
    const = lambda bi, j: (0, 0)
    once = pl.Buffered(1)
    params = pltpu.CompilerParams(dimension_semantics=("parallel", "arbitrary"),
                                  vmem_limit_bytes=VMEM_LIMIT)
    att = pl.pallas_call(
        _attn_kernel,
        out_shape=jax.ShapeDtypeStruct((b, s, ATT_WIDTH), BF16),
        grid=(b, nblk),
        in_specs=[
            pl.BlockSpec((1, T, ATT_WIDTH), blk),
            pl.BlockSpec((ATT_WIDTH, T), colblk),
            pl.BlockSpec((1, T, ATT_WIDTH), blk),
            pl.BlockSpec((ATT_HEADS + 1, ATT_TQ, ATT_BAND), lambda bi, j: (0, 0, 0),
                         pipeline_mode=once),
        ],
        out_specs=pl.BlockSpec((1, T, ATT_WIDTH), blk),
        scratch_shapes=[
            pltpu.VMEM((ATT_RING, ATT_HEADS * PAIR, ATT_TQ), BF16),
            pltpu.VMEM((ATT_RING, ATT_TQ, ATT_HEADS * PAIR), BF16),
        ],
        compiler_params=params,
        name="band_attn",
    )(q, kt, v, table)
    ssd = pl.pallas_call(
        _ssd_kernel,
        out_shape=jax.ShapeDtypeStruct((b, s, SSD_WIDTH), BF16),
        grid=(b, nblk),
        in_specs=[
            pl.BlockSpec((1, T, SSD_WIDTH), blk),
            pl.BlockSpec((1, T, CONV_DIM), blk),
            pl.BlockSpec((SSD_HEADS, T), colblk),
            pl.BlockSpec((SSD_CONV, CONV_DIM), const),
            pl.BlockSpec((1, CONV_DIM), const),
            pl.BlockSpec((1, LANES), const),
            pl.BlockSpec((1, LANES), const),
            pl.BlockSpec((1, SSD_WIDTH), const),
            pl.BlockSpec((SSD_CHUNKS, EXPAND_TERMS * LANES, SSD_WIDTH), lambda bi, j: (0, 0, 0),
                         pipeline_mode=once),
            pl.BlockSpec((SSD_L, 3 * SSD_L), const, pipeline_mode=once),
            pl.BlockSpec(((SSD_CONV - 1) * SSD_L, 2 * SSD_L), const, pipeline_mode=once),
        ],
        out_specs=pl.BlockSpec((1, T, SSD_WIDTH), blk),
        scratch_shapes=[
            pltpu.VMEM((T + SSD_L, CONV_DIM), BF16),
            pltpu.VMEM((SSD_GROUPS, SSD_STATE, SSD_WIDTH // SSD_GROUPS), F32),
        ],
        compiler_params=params,
        name="ssd_scan",
    )(gz, xbc, dtt_raw, conv_w, conv_b, dt_bias_p, a_log_p, d_exp, e3c, tril3, shift)
    return att, ssd


def _shift_matrix():
    r = jnp.arange((SSD_CONV - 1) * SSD_L)[:, None]
    c = jnp.arange(2 * SSD_L)[None, :]
    return (c == SSD_L + r % SSD_L - (r // SSD_L + 1)).astype(BF16)


def _tril3_matrix():
    r = jnp.arange(SSD_L)[:, None]
    c = jnp.arange(3 * SSD_L)[None, :] % SSD_L
    return (r >= c).astype(BF16)


def _expand_matrices():
    r = jnp.arange(EXPAND_TERMS * LANES)[None, :, None] % LANES
    c = jnp.arange(SSD_CHUNKS)[:, None, None]
    col = jnp.arange(SSD_WIDTH)[None, None, :]
    hit = (r // SSD_HEADS == c) & (r % SSD_HEADS == col // SSD_HEAD_DIM)
    return hit.astype(BF16)


def _out_ffn_kernel(x_ref, att_ref, ssd_ref, sw_ref, wo_ref, nw_ref,
                    w1_ref, w2_ref, fw_ref, o_ref):
    gw = SSD_WIDTH // SSD_GROUPS
    rt = x_ref.shape[0] // FFN_SPLIT
    rows = [slice(r * rt, (r + 1) * rt) for r in range(FFN_SPLIT)]

    def mix(rs):
        ssd = jnp.concatenate(
            [_rms(ssd_ref[rs, g * gw:(g + 1) * gw].astype(F32),
                  sw_ref[:, g * gw:(g + 1) * gw]).astype(BF16) for g in range(SSD_GROUPS)], axis=1)
        return (x_ref[rs, :]
                + jnp.dot(att_ref[rs, :], wo_ref[:ATT_WIDTH, :], preferred_element_type=F32)
                + jnp.dot(ssd, wo_ref[ATT_WIDTH:, :], preferred_element_type=F32))

    hs = [mix(rs) for rs in rows]
    hids = [jnp.dot(_rms(h, nw_ref[...]).astype(BF16), w1_ref[...], preferred_element_type=F32)
            for h in hs]
    for rs, h, hid in zip(rows, hs, hids):
        act = jnp.square(jnp.maximum(hid, 0.0)).astype(BF16)
        h = h + jnp.dot(act, w2_ref[...], preferred_element_type=F32)
        o_ref[rs, :] = _rms(h, fw_ref[...])


def _out_ffn(x2, att2, ssd2, ssd_norm_w, w_out, norm_w, w1, w2, final_w):
    m = x2.shape[0]
    tm = FFN_TM
    row = lambda i: (i, 0)
    const = lambda i: (0, 0)
    once = pl.Buffered(1)
    return pl.pallas_call(
        _out_ffn_kernel,
        out_shape=jax.ShapeDtypeStruct((m, D_MODEL), F32),
        grid=(m // tm,),
        in_specs=[
            pl.BlockSpec((tm, D_MODEL), row),
            pl.BlockSpec((tm, ATT_WIDTH), row),
            pl.BlockSpec((tm, SSD_WIDTH), row),
            pl.BlockSpec((1, SSD_WIDTH), const),
            pl.BlockSpec((ATT_WIDTH + SSD_WIDTH, D_MODEL), const, pipeline_mode=once),
            pl.BlockSpec((1, D_MODEL), const),
            pl.BlockSpec((D_MODEL, D_FF), const, pipeline_mode=once),
            pl.BlockSpec((D_FF, D_MODEL), const, pipeline_mode=once),
            pl.BlockSpec((1, D_MODEL), const),
        ],
        out_specs=pl.BlockSpec((tm, D_MODEL), row),
        compiler_params=pltpu.CompilerParams(
            dimension_semantics=("parallel",), vmem_limit_bytes=FFN_VMEM_LIMIT),
        name="out_ffn",
    )(x2, att2, ssd2, ssd_norm_w, w_out, norm_w, w1, w2, final_w)


def _layer(x, norm_mix_w, w_in, rel_bias, conv_w, conv_b, dt_bias, a_log, d_skip,
           ssd_norm_w, w_out, norm_mlp_w, w_ff1, w_ff2, out_norm_w):
    b, s, d = x.shape
    m = b * s
    x2 = x.reshape(m, d)

    scale = ATT_HEAD_DIM ** -0.5 * LOG2E
    col = jnp.arange(MAIN_PROJ + SSD_HEADS)[None, :]
    w_main = (w_in * jnp.where(col < ATT_WIDTH, scale, 1.0)).astype(BF16)
    w_ktd = jnp.concatenate([w_in[:, ATT_WIDTH:2 * ATT_WIDTH], w_in[:, MAIN_PROJ:]],
                            axis=1).T.astype(BF16)
    per_lane = lambda v: jnp.tile(v.astype(F32), SSD_CHUNKS).reshape(1, LANES)

    q, kt, v, gz, xbc, dtt_raw = _in_proj(x2, norm_mix_w.reshape(1, d), w_main, w_ktd)

    att, ssd = _mixers(q.reshape(b, s, -1), kt, v.reshape(b, s, -1), _bias_table(rel_bias),
                       gz.reshape(b, s, -1), xbc.reshape(b, s, -1), dtt_raw,
                       conv_w.astype(F32), conv_b.reshape(1, -1).astype(F32),
                       per_lane(dt_bias), per_lane(a_log),
                       jnp.repeat(d_skip.astype(F32), SSD_HEAD_DIM).reshape(1, -1),
                       _expand_matrices(), _tril3_matrix(), _shift_matrix())

    y = _out_ffn(x2, att.reshape(m, -1), ssd.reshape(m, -1),
                 ssd_norm_w.reshape(1, -1).astype(F32),
                 w_out.astype(BF16),
                 norm_mlp_w.reshape(1, d), w_ff1.astype(BF16), w_ff2.astype(BF16),
                 out_norm_w.reshape(1, d))
    return y.reshape(b, s, d)


def kernel(x, norm_mix_w, w_in, rel_bias, conv_w, conv_b, dt_bias, a_log, d_skip,
           ssd_norm_w, w_out, norm_mlp_w, w_ff1, w_ff2, norm_final_w):
    depth = w_in.shape[0]
    assert depth == 1, "final norm is fused into the single layer's last kernel"
    return _layer(x, norm_mix_w[0], w_in[0], rel_bias[0], conv_w[0], conv_b[0],
                  dt_bias[0], a_log[0], d_skip[0], ssd_norm_w[0], w_out[0],
                  norm_mlp_w[0], w_ff1[0], w_ff2[0], norm_final_w)
```
